```python
import math
import jax, jax.numpy as jnp
from jax import lax
import numpy as np


D_MODEL = 1024
BATCH = 2
SEQ = 16384
DEPTH = 4

CHUNK = 64
N_MIXERS = 2
N_HEADS = 16
HEAD_DIM = D_MODEL // N_HEADS
IDX_HEADS = 8
IDX_DIM = 64
TOPK_MAX = 256
Q_BLOCK = 128
ROPE_THETA = 10000.0
SSM_GROUP = 16
N_GROUPS = D_MODEL // SSM_GROUP
SSM_STATE = 64
D_FF = 2816
CONV_W = 3
EPS = 1e-6
NEG = -1e30
N_ATTN = (DEPTH + 1) // 2
N_SSM = DEPTH // 2
IN_COLS = 3 * D_MODEL + IDX_HEADS * IDX_DIM + IDX_DIM + IDX_HEADS

kernel_name = "chunk_causal_dsa_s5_hybrid"


def rms_norm(x, g):
    xf = x.astype(jnp.float32)
    y = xf * lax.rsqrt(jnp.mean(xf * xf, axis=-1, keepdims=True) + EPS)
    return (y * g.astype(jnp.float32)).astype(x.dtype)


def rope(x, positions):
    half = x.shape[-1] // 2
    inv = ROPE_THETA ** (-jnp.arange(half, dtype=jnp.float32) / half)
    ang = positions.astype(jnp.float32)[..., None] * inv
    cos = jnp.cos(ang)[:, :, None, :]
    sin = jnp.sin(ang)[:, :, None, :]
    xf = x.astype(jnp.float32)
    x1, x2 = xf[..., :half], xf[..., half:]
    return jnp.concatenate([x1 * cos - x2 * sin, x2 * cos + x1 * sin], axis=-1).astype(x.dtype)


def dsa_attention(h, positions, w_in, q_gain, k_gain, w_out):
    B, L, D = h.shape
    proj = h @ w_in
    s1 = D
    s2 = 2 * D
    s3 = 3 * D
    s4 = s3 + IDX_HEADS * IDX_DIM
    s5 = s4 + IDX_DIM
    q = proj[..., :s1].reshape(B, L, N_HEADS, HEAD_DIM)
    k = proj[..., s1:s2].reshape(B, L, N_HEADS, HEAD_DIM)
    v = proj[..., s2:s3].reshape(B, L, N_HEADS, HEAD_DIM)
    qi = proj[..., s3:s4].reshape(B, L, IDX_HEADS, IDX_DIM)
    ki = proj[..., s4:s5].reshape(B, L, 1, IDX_DIM)
    wi = proj[..., s5:].astype(jnp.float32) * IDX_HEADS ** -0.5
    q = rope(rms_norm(q, q_gain), positions)
    k = rope(rms_norm(k, k_gain), positions)
    qi = rope(qi, positions).astype(jnp.float32)
    ki = rope(ki, positions)[:, :, 0, :].astype(jnp.float32)
    topk = min(TOPK_MAX, L // 4)
    key_chunk = jnp.arange(L) // CHUNK
    n_blk = L // Q_BLOCK

    def block(i):
        start = i * Q_BLOCK
        qb = lax.dynamic_slice_in_dim(q, start, Q_BLOCK, axis=1)
        qib = lax.dynamic_slice_in_dim(qi, start, Q_BLOCK, axis=1)
        wib = lax.dynamic_slice_in_dim(wi, start, Q_BLOCK, axis=1)
        q_chunk = (start + jnp.arange(Q_BLOCK)) // CHUNK
        logits = jnp.einsum('bqhd,bsd->bqhs', qib, ki) * IDX_DIM ** -0.5
        score = jnp.einsum('bqh,bqhs->bqs', wib, jax.nn.relu(logits))
        admissible = key_chunk[None, :] <= q_chunk[:, None]
        score = jnp.where(admissible[None], score, NEG)
        _, idx = lax.top_k(score, topk)
        valid = (idx // CHUNK) <= q_chunk[None, :, None]
        kg = jax.vmap(lambda kb, ib: kb[ib])(k, idx)
        vg = jax.vmap(lambda vb, ib: vb[ib])(v, idx)
        s = jnp.einsum('bqhd,bqkhd->bqhk', qb.astype(jnp.float32), kg.astype(jnp.float32)) * HEAD_DIM ** -0.5
        s = jnp.where(valid[:, :, None, :], s, NEG)
        p = jax.nn.softmax(s, axis=-1)
        o = jnp.einsum('bqhk,bqkhd->bqhd', p.astype(v.dtype), vg)
        return o.reshape(B, Q_BLOCK, D)

    out = lax.map(block, jnp.arange(n_blk))
    out = out.transpose(1, 0, 2, 3).reshape(B, L, D)
    return out @ w_out


def _complex_scan_op(e1, e2):
    a1r, a1i, b1r, b1i = e1
    a2r, a2i, b2r, b2i = e2
    ar = a2r * a1r - a2i * a1i
    ai = a2r * a1i + a2i * a1r
    br = a2r * b1r - a2i * b1i + b2r
    bi = a2r * b1i + a2i * b1r + b2i
    return (ar, ai, br, bi)


def s5_mixer(h, a_re, a_im, log_dt, b_re, b_im, c_re, c_im, d_skip, w_glu):
    B, L, D = h.shape
    u = h.astype(jnp.float32).reshape(B, L, N_GROUPS, SSM_GROUP)
    a_re = a_re.astype(jnp.float32)
    a_im = a_im.astype(jnp.float32)
    dt = jnp.exp(log_dt.astype(jnp.float32))[:, None]
    decay = jnp.exp(a_re * dt)
    ab_re = decay * jnp.cos(a_im * dt)
    ab_im = decay * jnp.sin(a_im * dt)
    den = a_re * a_re + a_im * a_im
    nr = ab_re - 1.0
    ni = ab_im
    coef_re = ((nr * a_re + ni * a_im) / den)[..., None]
    coef_im = ((ni * a_re - nr * a_im) / den)[..., None]
    br = b_re.astype(jnp.float32)
    bim = b_im.astype(jnp.float32)
    bb_re = coef_re * br - coef_im * bim
    bb_im = coef_re * bim + coef_im * br
    bu_re = jnp.einsum('blgc,gpc->blgp', u, bb_re)
    bu_im = jnp.einsum('blgc,gpc->blgp', u, bb_im)
    ar = jnp.broadcast_to(ab_re, bu_re.shape)
    ai = jnp.broadcast_to(ab_im, bu_re.shape)
    _, _, x_re, x_im = lax.associative_scan(_complex_scan_op, (ar, ai, bu_re, bu_im), axis=1)
    y = (jnp.einsum('blgp,gcp->blgc', x_re, c_re.astype(jnp.float32))
         - jnp.einsum('blgp,gcp->blgc', x_im, c_im.astype(jnp.float32)))
    y = y.reshape(B, L, D) + d_skip.astype(jnp.float32) * u.reshape(B, L, D)
    g = jax.nn.gelu(y).astype(h.dtype)
    ga, gb = jnp.split(g @ w_glu, 2, axis=-1)
    return ga * jax.nn.sigmoid(gb)


def conv_ffn(h, w_up, conv_w, conv_b, w_down):
    up = h @ w_up
    up = lax.conv_general_dilated(
        up, conv_w[:, None, :], window_strides=(1,), padding=[(CONV_W - 1, 0)],
        dimension_numbers=('NWC', 'WIO', 'NWC'), feature_group_count=2 * D_FF) + conv_b
    val, gate = jnp.split(up, 2, axis=-1)
    return (jax.nn.silu(gate) * val) @ w_down


def setup_inputs(seed: int = 0) -> dict:
    key = jax.random.key(seed)
    ks = jax.random.split(key, 32)
    D = D_MODEL
    f32 = jnp.float32
    nrm = lambda k, shape, s: jax.random.normal(k, shape, f32) * s
    x = nrm(ks[0], (BATCH, SEQ, D), 1.0)
    c = nrm(ks[1], (BATCH, D), 1.0)
    offsets = jax.random.randint(ks[2], (BATCH, 1), 0, 4096, dtype=jnp.int32)
    positions = (offsets + jnp.arange(SEQ, dtype=jnp.int32)[None, :]).astype(jnp.int32)
    n_idx = jnp.arange(SSM_STATE, dtype=f32)
    return {
        "x": x,
        "c": c,
        "positions": positions,
        "ada_w": nrm(ks[3], (DEPTH, D, 6 * D), 0.5 * D ** -0.5),
        "ada_b": nrm(ks[4], (DEPTH, 6 * D), 0.02),
        "norm_mix": 1.0 + nrm(ks[5], (DEPTH, D), 0.02),
        "norm_ffn": 1.0 + nrm(ks[6], (DEPTH, D), 0.02),
        "attn_w_in": nrm(ks[7], (N_ATTN, D, IN_COLS), D ** -0.5),
        "attn_q_gain": 1.0 + nrm(ks[8], (N_ATTN, HEAD_DIM), 0.02),
        "attn_k_gain": 1.0 + nrm(ks[9], (N_ATTN, HEAD_DIM), 0.02),
        "attn_w_out": nrm(ks[10], (N_ATTN, D, D), D ** -0.5),
        "ssm_a_re": -0.5 * jnp.exp(nrm(ks[11], (N_SSM, N_GROUPS, SSM_STATE), 0.05)),
        "ssm_a_im": math.pi * n_idx + nrm(ks[12], (N_SSM, N_GROUPS, SSM_STATE), 0.05),
        "ssm_log_dt": jax.random.uniform(ks[13], (N_SSM, N_GROUPS), f32, math.log(1e-3), math.log(1e-1)),
        "ssm_b_re": nrm(ks[14], (N_SSM, N_GROUPS, SSM_STATE, SSM_GROUP), (2 * SSM_GROUP) ** -0.5),
        "ssm_b_im": nrm(ks[15], (N_SSM, N_GROUPS, SSM_STATE, SSM_GROUP), (2 * SSM_GROUP) ** -0.5),
        "ssm_c_re": nrm(ks[16], (N_SSM, N_GROUPS, SSM_GROUP, SSM_STATE), SSM_STATE ** -0.5),
        "ssm_c_im": nrm(ks[17], (N_SSM, N_GROUPS, SSM_GROUP, SSM_STATE), SSM_STATE ** -0.5),
        "ssm_d": nrm(ks[18], (N_SSM, D), 1.0),
        "ssm_w_glu": nrm(ks[19], (N_SSM, D, 2 * D), D ** -0.5),
        "ffn_w_up": nrm(ks[20], (DEPTH, D, 2 * D_FF), D ** -0.5),
        "ffn_conv_w": nrm(ks[21], (DEPTH, CONV_W, 2 * D_FF), CONV_W ** -0.5),
        "ffn_conv_b": nrm(ks[22], (DEPTH, 2 * D_FF), 0.02),
        "ffn_w_down": nrm(ks[23], (DEPTH, D_FF, D), D_FF ** -0.5),
    }


def reference(x, c, positions, ada_w, ada_b, norm_mix, norm_ffn, attn_w_in, attn_q_gain,
              attn_k_gain, attn_w_out, ssm_a_re, ssm_a_im, ssm_log_dt, ssm_b_re, ssm_b_im,
              ssm_c_re, ssm_c_im, ssm_d, ssm_w_glu, ffn_w_up, ffn_conv_w, ffn_conv_b, ffn_w_down):
    cond = jax.nn.silu(c)
    for i in range(DEPTH):
        mod = cond @ ada_w[i] + ada_b[i]
        sh_m, sc_m, g_m, sh_f, sc_f, g_f = [m[:, None, :] for m in jnp.split(mod, 6, axis=-1)]
        h = rms_norm(x, norm_mix[i]) * (1.0 + sc_m) + sh_m
        j = i // N_MIXERS
        if i % N_MIXERS == 0:
            y = dsa_attention(h, positions, attn_w_in[j], attn_q_gain[j], attn_k_gain[j], attn_w_out[j])
        else:
            y = s5_mixer(h, ssm_a_re[j], ssm_a_im[j], ssm_log_dt[j], ssm_b_re[j], ssm_b_im[j],
                         ssm_c_re[j], ssm_c_im[j], ssm_d[j], ssm_w_glu[j])
        x = x + g_m * y
        h = rms_norm(x, norm_ffn[i]) * (1.0 + sc_f) + sh_f
        x = x + g_f * conv_ffn(h, ffn_w_up[i], ffn_conv_w[i], ffn_conv_b[i], ffn_w_down[i])
    return x
```

```python
import functools
import math

import jax
import jax.numpy as jnp
from jax import lax
from jax.experimental import pallas as pl
from jax.experimental.pallas import tpu as pltpu

F32 = jnp.float32
BF16 = jnp.bfloat16
I32 = jnp.int32

LANES = 128
SUBLANES = 8
VMEM_LIMIT = 56 << 20

D_MODEL = 1024
N_HEADS = 16
HEAD_DIM = 64
IDX_HEADS = 8
IDX_DIM = 64
CHUNK = 64
TOPK_MAX = 256
ROPE_THETA = 10000.0
SSM_GROUP = 16
N_GROUPS = D_MODEL // SSM_GROUP
SSM_STATE = 64
D_FF = 2816
EPS = 1e-6
NEG = -1e30
IN_COLS = 3 * D_MODEL + IDX_HEADS * IDX_DIM + IDX_DIM + IDX_HEADS
IN_COLS_PAD = 29 * LANES
SSM_T = 128


def _cparams(*sem):
    return pltpu.CompilerParams(dimension_semantics=sem, vmem_limit_bytes=VMEM_LIMIT)


def _sigmoid(x):
    return 1.0 / (1.0 + jnp.exp(-x))


def _norm_mod(x, g, sc, sh):
    ms = jnp.mean(x * x, axis=-1, keepdims=True)
    return (x * lax.rsqrt(ms + EPS) * g) * (1.0 + sc) + sh


def _bdot(a, b):
    return jnp.dot(a, b, preferred_element_type=F32)


def _mod_kernel(c_ref, w_ref, b_ref, o_ref):
    c = c_ref[...]
    cond = c * _sigmoid(c)
    o_ref[0] = _bdot(cond.astype(BF16), w_ref[0].astype(BF16)) + b_ref[0]


def _modulation(c, ada_w, ada_b):
    depth, d, six_d = ada_w.shape
    b = c.shape[0]
    c_pad = jnp.zeros((SUBLANES, d), F32).at[:b].set(c)
    out = pl.pallas_call(
        _mod_kernel,
        grid=(depth, six_d // d),
        in_specs=[
            pl.BlockSpec((SUBLANES, d), lambda i, j: (0, 0)),
            pl.BlockSpec((1, d, d), lambda i, j: (i, 0, j)),
            pl.BlockSpec((1, 1, d), lambda i, j: (i, 0, j)),
        ],
        out_specs=pl.BlockSpec((1, SUBLANES, d), lambda i, j: (i, 0, j)),
        out_shape=jax.ShapeDtypeStruct((depth, SUBLANES, six_d), F32),
        compiler_params=_cparams("parallel", "parallel"),
        name="adaln_mod",
    )(c_pad, ada_w, ada_b.reshape(depth, 1, six_d))
    return out[:, :b, :].reshape(depth, b, 6, 1, d)


def _vec_spec(rows_per_batch_tiles):
    return pl.BlockSpec((1, 1, D_MODEL), lambda i: (i // rows_per_batch_tiles, 0, 0))


def _const_spec(shape):
    nd = len(shape)
    return pl.BlockSpec(shape, lambda i: (0,) * nd)


FFN_TM = 512
FFN_FC = 256


def _ffn_kernel(x_ref, xp_ref, g_ref, sc_ref, sh_ref, gate_ref, wv_ref, wg_ref, cw_ref, cb_ref,
                wd_ref, o_ref, *, tiles_per_batch):
    i = pl.program_id(0)
    x = x_ref[...]
    g, sc, sh = g_ref[...], sc_ref[0], sh_ref[0]
    h = _norm_mod(x, g, sc, sh).astype(BF16)
    hp = _norm_mod(xp_ref[...], g, sc, sh).astype(BF16)
    keep_prev = (i % tiles_per_batch != 0).astype(F32)
    tm = x.shape[0]
    row8 = lax.broadcasted_iota(I32, (SUBLANES, FFN_FC), 0)

    def conv(w_ref, f, col0):
        cs = slice(f * FFN_FC, (f + 1) * FFN_FC)
        up = _bdot(h, w_ref[:, cs])
        upp = _bdot(hp, w_ref[:, cs]) * keep_prev
        p7 = upp[7:8, :]
        p6 = upp[6:7, :]
        up1 = pltpu.roll(up, 1, 0)
        up2 = pltpu.roll(up, 2, 0)
        top1 = jnp.where(row8 == 0, p7, up1[:SUBLANES])
        top2 = jnp.where(row8 == 0, p6, jnp.where(row8 == 1, p7, up2[:SUBLANES]))
        up1 = jnp.concatenate([top1, up1[SUBLANES:]], axis=0)
        up2 = jnp.concatenate([top2, up2[SUBLANES:]], axis=0)
        wsl = slice(col0 + f * FFN_FC, col0 + (f + 1) * FFN_FC)
        cw = cw_ref[:, wsl]
        return up2 * cw[0:1] + up1 * cw[1:2] + up * cw[2:3] + cb_ref[:, wsl]

    acc = jnp.zeros((tm, D_MODEL), F32)
    for f in range(D_FF // FFN_FC):
        val = conv(wv_ref, f, 0)
        gt = conv(wg_ref, f, D_FF)
        act = (gt * _sigmoid(gt) * val).astype(BF16)
        acc = acc + _bdot(act, wd_ref[f * FFN_FC:(f + 1) * FFN_FC, :])
    o_ref[...] = x + gate_ref[0] * acc


def _conv_ffn(x, norm_g, sc, sh, gate, w_val, w_gate, conv_w, conv_b, w_down, seq_len):
    n, d = x.shape
    tm = min(FFN_TM, seq_len)
    tpb = seq_len // tm
    kern = functools.partial(_ffn_kernel, tiles_per_batch=tpb)
    return pl.pallas_call(
        kern,
        grid=(n // tm,),
        in_specs=[
            pl.BlockSpec((tm, d), lambda i: (i, 0)),
            pl.BlockSpec((SUBLANES, d), lambda i: (jnp.maximum(i * (tm // SUBLANES) - 1, 0), 0)),
            _const_spec((1, d)),
            _vec_spec(tpb), _vec_spec(tpb), _vec_spec(tpb),
            _const_spec((d, D_FF)), _const_spec((d, D_FF)),
            _const_spec((3, 2 * D_FF)), _const_spec((1, 2 * D_FF)),
            _const_spec((D_FF, d)),
        ],
        out_specs=pl.BlockSpec((tm, d), lambda i: (i, 0)),
        out_shape=jax.ShapeDtypeStruct((n, d), F32),
        compiler_params=_cparams("parallel"),
        name="conv_ffn",
    )(x, x, norm_g, sc, sh, gate, w_val, w_gate, conv_w, conv_b, w_down)


ROPE_TM = 512


def _rope_kernel(pos_ref, inv_ref, cos_ref, sa_ref, sb_ref):
    ang = pos_ref[...].astype(F32) * inv_ref[...]
    cs = jnp.cos(ang)
    sn = jnp.sin(ang)
    lane = lax.broadcasted_iota(I32, ang.shape, 1)
    low = (lane % HEAD_DIM) < (HEAD_DIM // 2)
    cos_ref[...] = cs
    sa_ref[...] = jnp.where(low, -sn, 0.0)
    sb_ref[...] = jnp.where(low, 0.0, sn)


def _rope_tables(positions):
    n = positions.size
    tm = min(ROPE_TM, n)
    half = HEAD_DIM // 2
    inv = ROPE_THETA ** (-jnp.arange(half, dtype=F32) / half)
    inv = jnp.tile(inv, LANES // half).reshape(1, LANES)
    spec = pl.BlockSpec((tm, LANES), lambda i: (i, 0))
    return pl.pallas_call(
        _rope_kernel,
        grid=(n // tm,),
        in_specs=[pl.BlockSpec((tm, 1), lambda i: (i, 0)), _const_spec((1, LANES))],
        out_specs=[spec, spec, spec],
        out_shape=[jax.ShapeDtypeStruct((n, LANES), F32)] * 3,
        compiler_params=_cparams("parallel"),
        name="rope_tables",
    )(positions.reshape(n, 1), inv)


def _rope(x, cos_t, sin_a, sin_b):
    return (x * cos_t + pltpu.roll(x, LANES - HEAD_DIM // 2, 1) * sin_a
            + pltpu.roll(x, HEAD_DIM // 2, 1) * sin_b)


PROJ_TM = 256


def _attn_proj_kernel(x_ref, g_ref, sc_ref, sh_ref, w_ref, qg_ref, kg_ref, cos_ref, sa_ref, sb_ref,
                      hm_ref, q_ref, kt_ref, v_ref, qi_ref, kit_ref, tail_ref):
    h = _norm_mod(x_ref[...], g_ref[...], sc_ref[0], sh_ref[0]).astype(BF16)
    cos_t, sin_a, sin_b = cos_ref[...], sa_ref[...], sb_ref[...]
    hm = hm_ref[...]
    n_qk = D_MODEL // LANES

    def group(gi):
        return _bdot(h, w_ref[:, gi * LANES:(gi + 1) * LANES])

    def head_norm(p, gain):
        sq = p * p
        hi = sq.astype(BF16)
        lo = (sq - hi.astype(F32)).astype(BF16)
        ms = _bdot(hi, hm) + _bdot(lo, hm)
        return p * lax.rsqrt(ms + EPS) * gain

    for gi in range(n_qk):
        q = _rope(head_norm(group(gi), qg_ref[...]), cos_t, sin_a, sin_b)
        q_ref[:, gi * LANES:(gi + 1) * LANES] = (q * HEAD_DIM ** -0.5).astype(BF16)
        k = _rope(head_norm(group(n_qk + gi), kg_ref[...]), cos_t, sin_a, sin_b)
        kt_ref[gi * LANES:(gi + 1) * LANES, :] = k.T.astype(BF16)
        v_ref[:, gi * LANES:(gi + 1) * LANES] = group(2 * n_qk + gi).astype(BF16)
    n_qi = IDX_HEADS * IDX_DIM // LANES
    for gi in range(n_qi):
        qi = _rope(group(3 * n_qk + gi), cos_t, sin_a, sin_b)
        qi_ref[:, gi * LANES:(gi + 1) * LANES] = qi.astype(BF16)
    tail = group(3 * n_qk + n_qi)
    ki = _rope(tail, cos_t, sin_a, sin_b)
    lane = lax.broadcasted_iota(I32, ki.shape, 1)
    ki2 = jnp.where(lane < IDX_DIM, ki, pltpu.roll(ki, IDX_DIM, 1))
    kit_ref[...] = ki2.T.astype(BF16)
    tail_ref[...] = tail * (IDX_HEADS ** -0.5 * IDX_DIM ** -0.5)


def _attn_proj(x, norm_g, sc, sh, w_in, q_gain, k_gain, cos_t, sin_a, sin_b, seq_len):
    n, d = x.shape
    tm = min(PROJ_TM, seq_len)
    tpb = seq_len // tm
    lane = jnp.arange(LANES)
    hm = ((lane[:, None] // HEAD_DIM) == (lane[None, :] // HEAD_DIM)).astype(BF16) / HEAD_DIM
    qg2 = jnp.tile(q_gain, LANES // HEAD_DIM).reshape(1, LANES)
    kg2 = jnp.tile(k_gain, LANES // HEAD_DIM).reshape(1, LANES)
    row = lambda w: pl.BlockSpec((tm, w), lambda i: (i, 0))
    col = lambda h: pl.BlockSpec((h, tm), lambda i: (0, i))
    n_idx = IDX_HEADS * IDX_DIM
    return pl.pallas_call(
        _attn_proj_kernel,
        grid=(n // tm,),
        in_specs=[
            row(d), _const_spec((1, d)), _vec_spec(tpb), _vec_spec(tpb),
            _const_spec((d, IN_COLS_PAD)), _const_spec((1, LANES)), _const_spec((1, LANES)),
            row(LANES), row(LANES), row(LANES), _const_spec((LANES, LANES)),
        ],
        out_specs=[row(d), col(d), row(d), row(n_idx), col(LANES), row(LANES)],
        out_shape=[
            jax.ShapeDtypeStruct((n, d), BF16),
            jax.ShapeDtypeStruct((d, n), BF16),
            jax.ShapeDtypeStruct((n, d), BF16),
            jax.ShapeDtypeStruct((n, n_idx), BF16),
            jax.ShapeDtypeStruct((LANES, n), BF16),
            jax.ShapeDtypeStruct((n, LANES), F32),
        ],
        compiler_params=_cparams("parallel"),
        name="attn_proj",
    )(x, norm_g, sc, sh, w_in, qg2, kg2, cos_t, sin_a, sin_b, hm)


ATT_QB = 128
ATT_TK = 512
INT_MIN = -(2 ** 31)


def _sort_key(s):
    bits = pltpu.bitcast(s + 0.0, I32)
    return bits ^ ((bits >> 31) & 0x7FFFFFFF)


def _attn_kernel(q_ref, kt_ref, v_ref, qi_ref, kit_ref, tail_ref, o_ref, key_ref, bias_ref,
                 *, seq_len, topk):
    qb = pl.program_id(1)
    hp = pl.program_id(2)
    nq, tk = ATT_QB, ATT_TK
    q0 = qb * nq
    n_tiles = (q0 + nq + tk - 1) // tk
    n_out = seq_len - n_tiles * tk
    lane = lax.broadcasted_iota(I32, (nq, LANES), 1)
    key_neg = _sort_key(jnp.full((nq, LANES), NEG, F32))

    @pl.when(hp == 0)
    def _select():
        qrow_chunk = (q0 + lax.broadcasted_iota(I32, (nq, tk), 0)) // CHUNK
        col = lax.broadcasted_iota(I32, (nq, tk), 1)
        tail = tail_ref[...]
        qm = []
        for h in range(IDX_HEADS):
            pair = qi_ref[:, (h // 2) * LANES:(h // 2 + 1) * LANES]
            in_head = (lane // IDX_DIM) == (h % 2)
            qm.append(jnp.where(in_head, pair, jnp.zeros_like(pair)))
        wi = [tail[:, IDX_DIM + h:IDX_DIM + h + 1] for h in range(IDX_HEADS)]

        def score_tile(j, carry):
            c0 = pl.multiple_of(j * tk, tk)
            kit = kit_ref[:, pl.ds(c0, tk)]
            s = jnp.zeros((nq, tk), F32)
            for h in range(IDX_HEADS):
                s = s + wi[h] * jnp.maximum(_bdot(qm[h], kit), 0.0)
            adm = ((c0 + col) // CHUNK) <= qrow_chunk
            s = jnp.where(adm, s, NEG)
            key_ref[:, pl.ds(c0, tk)] = _sort_key(s)
            return carry

        lax.fori_loop(0, n_tiles, score_tile, 0)

        def count(ind_fn):
            def body(j, cnt):
                c0 = pl.multiple_of(j * tk, tk)
                for u in range(tk // LANES):
                    ks = key_ref[:, pl.ds(c0 + u * LANES, LANES)]
                    cnt = cnt + ind_fn(ks, c0 + u * LANES)
                return cnt
            cnt = lax.fori_loop(0, n_tiles, body, jnp.zeros((nq, LANES), I32))
            tot = jnp.sum(cnt.astype(F32), axis=1, keepdims=True).astype(I32)
            return jnp.broadcast_to(tot, (nq, LANES))

        def bit_step(b, t):
            bit = jnp.left_shift(jnp.int32(1), 31 - b)
            cand = jnp.where(b == 0, jnp.zeros_like(t), t | bit)
            cnt = (count(lambda ks, c0: jnp.where(ks >= cand, 1, 0))
                   + jnp.where(cand <= key_neg, n_out, 0))
            return jnp.where(cnt >= topk, cand, t)

        t = lax.fori_loop(0, 32, bit_step, jnp.full((nq, LANES), INT_MIN, I32))
        n_gt = count(lambda ks, c0: jnp.where(ks > t, 1, 0)) + jnp.where(t < key_neg, n_out, 0)
        n_eq = count(lambda ks, c0: jnp.where(ks == t, 1, 0))
        need = topk - n_gt
        real = jnp.where(t > key_neg, 1, 0)
        cut_all = real * seq_len

        def tie_cut():
            def idx_step(b, c):
                cand = c | jnp.left_shift(jnp.int32(1), 14 - b)
                f = count(lambda ks, c0: jnp.where(ks == t, jnp.where((c0 + lane) < cand, 1, 0), 0))
                return jnp.where(f <= need, cand, c)
            c = lax.fori_loop(0, 15, idx_step, jnp.zeros((nq, LANES), I32))
            return real * c

        n_amb = jnp.sum((real * jnp.where(n_eq > need, 1, 0)).astype(F32))
        cut = lax.cond(n_amb > 0.0, tie_cut, lambda: cut_all)

        def bias_tile(j, carry):
            c0 = pl.multiple_of(j * tk, tk)
            for u in range(tk // LANES):
                cu = c0 + u * LANES
                ks = key_ref[:, pl.ds(cu, LANES)]
                tie_ok = jnp.where((cu + lane) < cut, 0.0, NEG)
                bias_ref[:, pl.ds(cu, LANES)] = jnp.where(
                    ks > t, 0.0, jnp.where(ks == t, tie_ok, NEG))
            return carry

        lax.fori_loop(0, n_tiles, bias_tile, 0)

    qp = q_ref[...]
    outs = []
    for hh in range(2):
        in_head = (lane // HEAD_DIM) == hh
        qh = jnp.where(in_head, qp, jnp.zeros_like(qp))

        def kv_tile(j, carry):
            m, l, acc = carry
            c0 = pl.multiple_of(j * tk, tk)
            s = _bdot(qh, kt_ref[:, pl.ds(c0, tk)]) + bias_ref[:, pl.ds(c0, tk)]
            m_new = jnp.maximum(m, jnp.max(s, axis=1, keepdims=True))
            alpha = jnp.exp(m - m_new)
            p = jnp.exp(s - m_new)
            l = alpha * l + jnp.sum(p, axis=1, keepdims=True)
            acc = alpha * acc + _bdot(p.astype(BF16), v_ref[pl.ds(c0, tk), :])
            return m_new, l, acc

        init = (jnp.full((nq, 1), NEG, F32), jnp.zeros((nq, 1), F32), jnp.zeros((nq, LANES), F32))
        _, l, acc = lax.fori_loop(0, n_tiles, kv_tile, init)
        outs.append(acc / l)
    o_ref[...] = jnp.where((lane // HEAD_DIM) == 0, outs[0], outs[1]).astype(BF16)


def _dsa_attention(q, kt, v, qi, kit, tail, batch, seq_len):
    n, d = q.shape
    nq = ATT_QB
    qpb = seq_len // nq
    n_idx = IDX_HEADS * IDX_DIM
    topk = min(TOPK_MAX, seq_len // 4)
    kern = functools.partial(_attn_kernel, seq_len=seq_len, topk=topk)
    return pl.pallas_call(
        kern,
        grid=(batch, qpb, d // LANES),
        in_specs=[
            pl.BlockSpec((nq, LANES), lambda b, i, h: (b * qpb + i, h)),
            pl.BlockSpec((LANES, seq_len), lambda b, i, h: (h, b)),
            pl.BlockSpec((seq_len, LANES), lambda b, i, h: (b, h)),
            pl.BlockSpec((nq, n_idx), lambda b, i, h: (b * qpb + i, 0)),
            pl.BlockSpec((LANES, seq_len), lambda b, i, h: (0, b)),
            pl.BlockSpec((nq, LANES), lambda b, i, h: (b * qpb + i, 0)),
        ],
        out_specs=pl.BlockSpec((nq, LANES), lambda b, i, h: (b * qpb + i, h)),
        out_shape=jax.ShapeDtypeStruct((n, d), BF16),
        scratch_shapes=[pltpu.VMEM((nq, seq_len), I32), pltpu.VMEM((nq, seq_len), F32)],
        compiler_params=_cparams("parallel", "parallel", "arbitrary"),
        name="dsa_attention",
    )(q, kt, v, qi, kit, tail)


OUT_TM = 512


def _out_proj_kernel(x_ref, a_ref, w_ref, gate_ref, o_ref):
    o_ref[...] = x_ref[...] + gate_ref[0] * _bdot(a_ref[...], w_ref[...])


def _out_proj(x, a, w, gate, seq_len):
    n, d = x.shape
    tm = min(OUT_TM, seq_len)
    tpb = seq_len // tm
    row = pl.BlockSpec((tm, d), lambda i: (i, 0))
    return pl.pallas_call(
        _out_proj_kernel,
        grid=(n // tm,),
        in_specs=[row, row, _const_spec((d, d)), _vec_spec(tpb)],
        out_specs=row,
        out_shape=jax.ShapeDtypeStruct((n, d), F32),
        compiler_params=_cparams("parallel"),
        name="attn_out_proj",
    )(x, a, w, gate)


NORMT_TM = 512


def _norm_t_kernel(x_ref, g_ref, sc_ref, sh_ref, o_ref):
    o_ref[...] = _norm_mod(x_ref[...], g_ref[...], sc_ref[0], sh_ref[0]).T


def _norm_transposed(x, norm_g, sc, sh, seq_len):
    n, d = x.shape
    tm = min(NORMT_TM, seq_len)
    tpb = seq_len // tm
    return pl.pallas_call(
        _norm_t_kernel,
        grid=(n // tm,),
        in_specs=[pl.BlockSpec((tm, d), lambda i: (i, 0)), _const_spec((1, d)),
                  _vec_spec(tpb), _vec_spec(tpb)],
        out_specs=pl.BlockSpec((d, tm), lambda i: (0, i)),
        out_shape=jax.ShapeDtypeStruct((d, n), F32),
        compiler_params=_cparams("parallel"),
        name="ssm_norm_t",
    )(x, norm_g, sc, sh)


def _ssm_prep_kernel(are_ref, aim_ref, ldt_ref, brt_ref, bit_ref, cr_ref, ci_ref, crt_ref, cit_ref,
                     toep_ref, bst_ref, cout_ref, lam_ref, kt_ref):
    t_len, p, sg = SSM_T, SSM_STATE, SSM_GROUP
    a_re, a_im = are_ref[0], aim_ref[0]
    dt = jnp.exp(ldt_ref[0])
    ar, th = a_re * dt, a_im * dt
    decay = jnp.exp(ar)
    ab_re, ab_im = decay * jnp.cos(th), decay * jnp.sin(th)
    den = a_re * a_re + a_im * a_im
    nr, ni = ab_re - 1.0, ab_im
    coef_re = (nr * a_re + ni * a_im) / den
    coef_im = (ni * a_re - nr * a_im) / den
    brt, bit = brt_ref[0], bit_ref[0]
    bbr = coef_re * brt - coef_im * bit
    bbi = coef_re * bit + coef_im * brt
    cr, ci = cr_ref[0], ci_ref[0]
    crt, cit = crt_ref[0], cit_ref[0]

    def powers(nn):
        mag = jnp.exp(nn * ar)
        return mag * jnp.cos(nn * th), mag * jnp.sin(nn * th)

    n_iota = lax.broadcasted_iota(I32, (t_len, p), 0).astype(F32)
    l_re, l_im = powers(n_iota)
    l1_re, l1_im = powers(n_iota + 1.0)
    lr_re, lr_im = powers((t_len - 1.0) - n_iota)
    lcat_t = jnp.concatenate([l_re, l_im], axis=1).T
    l1cat_t = jnp.concatenate([l1_re, l1_im], axis=1).T
    l1_re_t, l1_im_t = l1cat_t[:p], l1cat_t[p:]

    mt = []
    for c in range(sg):
        m_re = cr[c:c + 1] * bbr - ci[c:c + 1] * bbi
        m_im = -(cr[c:c + 1] * bbi + ci[c:c + 1] * bbr)
        mt.append(jnp.concatenate([m_re, m_im], axis=1))
    mt = jnp.concatenate(mt, axis=0)
    kt_ref[...] = jnp.dot(mt, lcat_t, preferred_element_type=F32, precision=lax.Precision.HIGHEST)

    tau = lax.broadcasted_iota(I32, (t_len, t_len), 0)
    tt = lax.broadcasted_iota(I32, (t_len, t_len), 1)
    causal = tt >= tau

    def toep_tile(r, carry):
        c = r // sg
        cp = r % sg
        krow = jnp.broadcast_to(kt_ref[pl.ds(r, 1), :], (t_len, t_len))
        tile = pltpu.roll(krow, 0, 1, stride=1, stride_axis=0)
        tile = jnp.where(causal, tile, 0.0).astype(BF16)
        toep_ref[0, pl.ds(pl.multiple_of(cp * t_len, t_len), t_len),
                 pl.ds(pl.multiple_of(c * t_len, t_len), t_len)] = tile
        return carry

    lax.fori_loop(0, sg * sg, toep_tile, 0)

    for c in range(sg):
        bst_ref[0, c * t_len:(c + 1) * t_len, :] = jnp.concatenate(
            [lr_re * bbr[c:c + 1] - lr_im * bbi[c:c + 1],
             lr_re * bbi[c:c + 1] + lr_im * bbr[c:c + 1]], axis=1).astype(BF16)
        ccr, cci = crt[:, c:c + 1], cit[:, c:c + 1]
        cout_ref[0, :, c * t_len:(c + 1) * t_len] = jnp.concatenate(
            [ccr * l1_re_t - cci * l1_im_t, -ccr * l1_im_t - cci * l1_re_t], axis=0).astype(BF16)

    lt_re, lt_im = powers(jnp.full((SUBLANES, p), float(t_len), F32))
    lam_ref[0] = jnp.concatenate([lt_re, lt_im], axis=1)


def _ssm_prep(a_re, a_im, log_dt, b_re, b_im, c_re, c_im):
    g, p, sg, t_len = N_GROUPS, SSM_STATE, SSM_GROUP, SSM_T
    vec = lambda a: a.reshape(g, 1, -1)
    spec3 = lambda s: pl.BlockSpec((1,) + s, lambda i: (i, 0, 0))
    return pl.pallas_call(
        _ssm_prep_kernel,
        grid=(g,),
        in_specs=[spec3((1, p)), spec3((1, p)), spec3((1, 1)), spec3((sg, p)), spec3((sg, p)),
                  spec3((sg, p)), spec3((sg, p)), spec3((p, sg)), spec3((p, sg))],
        out_specs=[spec3((sg * t_len, sg * t_len)), spec3((sg * t_len, 2 * p)),
                   spec3((2 * p, sg * t_len)), spec3((SUBLANES, 2 * p))],
        out_shape=[
            jax.ShapeDtypeStruct((g, sg * t_len, sg * t_len), BF16),
            jax.ShapeDtypeStruct((g, sg * t_len, 2 * p), BF16),
            jax.ShapeDtypeStruct((g, 2 * p, sg * t_len), BF16),
            jax.ShapeDtypeStruct((g, SUBLANES, 2 * p), F32),
        ],
        scratch_shapes=[pltpu.VMEM((sg * sg, t_len), F32)],
        compiler_params=_cparams("parallel"),
        name="ssm_prep",
    )(vec(a_re), vec(a_im), vec(log_dt), b_re.transpose(0, 2, 1), b_im.transpose(0, 2, 1),
      c_re, c_im, c_re.transpose(0, 2, 1), c_im.transpose(0, 2, 1))


def _gelu_tanh(x):
    return 0.5 * x * (1.0 + jnp.tanh(math.sqrt(2.0 / math.pi) * (x + 0.044715 * (x * x * x))))


def _ssm_kernel(u_ref, toep_ref, bst_ref, cout_ref, lam_ref, d_ref, o_ref, *, chunks_per_batch):
    sg, t_len, p = SSM_GROUP, SSM_T, SSM_STATE
    nc = u_ref.shape[1]
    lhs = jnp.concatenate([u_ref[c].astype(BF16) for c in range(sg)], axis=1)
    state = _bdot(lhs, bst_ref[0])
    row = lax.broadcasted_iota(I32, (nc, 2 * p), 0) % chunks_per_batch
    lane = lax.broadcasted_iota(I32, (1, 2 * p), 1)
    lam = lam_ref[0]
    a_re = lam[0:1, :p]
    a_im = lam[0:1, p:]

    def cmul_rows(a_re, a_im, x):
        a1 = jnp.concatenate([a_re, a_re], axis=1)
        a2 = jnp.concatenate([-a_im, a_im], axis=1)
        return a1 * x + a2 * pltpu.roll(x, p, 1)

    k = 1
    while k < chunks_per_batch:
        shifted = jnp.where(row >= k, pltpu.roll(state, k, 0), 0.0)
        state = state + cmul_rows(a_re, a_im, shifted)
        a_re, a_im = a_re * a_re - a_im * a_im, 2.0 * a_re * a_im
        k *= 2
    carry = jnp.where(row >= 1, pltpu.roll(state, 1, 0), 0.0)
    del lane
    y = _bdot(lhs, toep_ref[0]) + _bdot(carry.astype(BF16), cout_ref[0])
    for c in range(sg):
        yc = y[:, c * t_len:(c + 1) * t_len] + d_ref[0, c:c + 1, :] * u_ref[c]
        o_ref[c] = _gelu_tanh(yc).astype(BF16)


def _ssm_apply(u_t, toep, bst, cout, lam, d_skip, seq_len):
    d, n = u_t.shape
    g, sg, t_len, p = N_GROUPS, SSM_GROUP, SSM_T, SSM_STATE
    nc = n // t_len
    u3 = u_t.reshape(d, nc, t_len)
    d3 = jnp.broadcast_to(d_skip.reshape(g, sg, 1), (g, sg, t_len))
    kern = functools.partial(_ssm_kernel, chunks_per_batch=seq_len // t_len)
    spec3 = lambda s: pl.BlockSpec((1,) + s, lambda i: (i, 0, 0))
    blk = pl.BlockSpec((sg, nc, t_len), lambda i: (i, 0, 0))
    out = pl.pallas_call(
        kern,
        grid=(g,),
        in_specs=[blk, spec3((sg * t_len, sg * t_len)), spec3((sg * t_len, 2 * p)),
                  spec3((2 * p, sg * t_len)), spec3((SUBLANES, 2 * p)), spec3((sg, t_len))],
        out_specs=blk,
        out_shape=jax.ShapeDtypeStruct((d, nc, t_len), BF16),
        compiler_params=_cparams("parallel"),
        name="ssm_apply",
    )(u3, toep, bst, cout, lam, d3)
    return out.reshape(d, n)


GLU_TM = 512


def _glu_kernel(x_ref, gt_ref, w_ref, gate_ref, o_ref):
    z = lax.dot_general(gt_ref[...], w_ref[...], (((0,), (0,)), ((), ())),
                        preferred_element_type=F32)
    d = x_ref.shape[1]
    o_ref[...] = x_ref[...] + gate_ref[0] * (z[:, :d] * _sigmoid(z[:, d:]))


def _glu_residual(x, g_t, w_glu, gate, seq_len):
    n, d = x.shape
    tm = min(GLU_TM, seq_len)
    tpb = seq_len // tm
    row = pl.BlockSpec((tm, d), lambda i: (i, 0))
    return pl.pallas_call(
        _glu_kernel,
        grid=(n // tm,),
        in_specs=[row, pl.BlockSpec((d, tm), lambda i: (0, i)), _const_spec((d, 2 * d)),
                  _vec_spec(tpb)],
        out_specs=row,
        out_shape=jax.ShapeDtypeStruct((n, d), F32),
        compiler_params=_cparams("parallel"),
        name="ssm_glu",
    )(x, g_t, w_glu, gate)


def kernel(x, c, positions, ada_w, ada_b, norm_mix, norm_ffn, attn_w_in, attn_q_gain, attn_k_gain,
           attn_w_out, ssm_a_re, ssm_a_im, ssm_log_dt, ssm_b_re, ssm_b_im, ssm_c_re, ssm_c_im,
           ssm_d, ssm_w_glu, ffn_w_up, ffn_conv_w, ffn_conv_b, ffn_w_down):
    batch, seq_len, d = x.shape
    depth = ada_w.shape[0]
    n = batch * seq_len
    xs = x.reshape(n, d)
    mod = _modulation(c, ada_w, ada_b)
    cos_t, sin_a, sin_b = _rope_tables(positions)
    for i in range(depth):
        sh_m, sc_m, g_m, sh_f, sc_f, g_f = [mod[i, :, k] for k in range(6)]
        j = i // 2
        if i % 2 == 0:
            w_in = jnp.pad(attn_w_in[j], ((0, 0), (0, IN_COLS_PAD - IN_COLS))).astype(BF16)
            q, kt, v, qi, kit, tail = _attn_proj(
                xs, norm_mix[i].reshape(1, d), sc_m, sh_m, w_in, attn_q_gain[j], attn_k_gain[j],
                cos_t, sin_a, sin_b, seq_len)
            o = _dsa_attention(q, kt, v, qi, kit, tail, batch, seq_len)
            xs = _out_proj(xs, o, attn_w_out[j].astype(BF16), g_m, seq_len)
        else:
            u_t = _norm_transposed(xs, norm_mix[i].reshape(1, d), sc_m, sh_m, seq_len)
            toep, bst, cout, lam = _ssm_prep(ssm_a_re[j], ssm_a_im[j], ssm_log_dt[j], ssm_b_re[j],
                                             ssm_b_im[j], ssm_c_re[j], ssm_c_im[j])
            g_t = _ssm_apply(u_t, toep, bst, cout, lam, ssm_d[j], seq_len)
            xs = _glu_residual(xs, g_t, ssm_w_glu[j].astype(BF16), g_m, seq_len)
        w_up = ffn_w_up[i].astype(BF16)
        xs = _conv_ffn(xs, norm_ffn[i].reshape(1, d), sc_f, sh_f, g_f, w_up[:, :D_FF], w_up[:, D_FF:],
                       ffn_conv_w[i], ffn_conv_b[i].reshape(1, -1), ffn_w_down[i].astype(BF16),
                       seq_len)
    return xs.reshape(batch, seq_len, d)
```

```python
import functools
import math

import jax
import jax.numpy as jnp
from jax import lax
from jax.experimental import pallas as pl
from jax.experimental.pallas import tpu as pltpu

F32 = jnp.float32
BF16 = jnp.bfloat16
I32 = jnp.int32

LANES = 128
SUBLANES = 8
VMEM_LIMIT = 56 << 20

D_MODEL = 1024
N_HEADS = 16
HEAD_DIM = 64
IDX_HEADS = 8
IDX_DIM = 64
CHUNK = 64
TOPK_MAX = 256
ROPE_THETA = 10000.0
SSM_GROUP = 16
N_GROUPS = D_MODEL // SSM_GROUP
SSM_STATE = 64
D_FF = 2816
EPS = 1e-6
NEG = -1e30
IN_COLS = 3 * D_MODEL + IDX_HEADS * IDX_DIM + IDX_DIM + IDX_HEADS
IN_COLS_PAD = 29 * LANES
SSM_T = 128


def _cparams(*sem):
    return pltpu.CompilerParams(dimension_semantics=sem, vmem_limit_bytes=VMEM_LIMIT)


def _sigmoid(x):
    return 1.0 / (1.0 + jnp.exp(-x))


def _norm_mod(x, g, sc, sh):
    ms = jnp.mean(x * x, axis=-1, keepdims=True)
    return (x * lax.rsqrt(ms + EPS) * g) * (1.0 + sc) + sh


def _bdot(a, b):
    return jnp.dot(a, b, preferred_element_type=F32)


def _mod_kernel(c_ref, w_ref, b_ref, o_ref):
    c = c_ref[...]
    cond = c * _sigmoid(c)
    o_ref[0] = _bdot(cond.astype(BF16), w_ref[0].astype(BF16)) + b_ref[0]


def _modulation(c, ada_w, ada_b):
    depth, d, six_d = ada_w.shape
    b = c.shape[0]
    c_pad = jnp.zeros((SUBLANES, d), F32).at[:b].set(c)
    out = pl.pallas_call(
        _mod_kernel,
        grid=(depth, six_d // d),
        in_specs=[
            pl.BlockSpec((SUBLANES, d), lambda i, j: (0, 0)),
            pl.BlockSpec((1, d, d), lambda i, j: (i, 0, j)),
            pl.BlockSpec((1, 1, d), lambda i, j: (i, 0, j)),
        ],
        out_specs=pl.BlockSpec((1, SUBLANES, d), lambda i, j: (i, 0, j)),
        out_shape=jax.ShapeDtypeStruct((depth, SUBLANES, six_d), F32),
        compiler_params=_cparams("parallel", "parallel"),
        name="adaln_mod",
    )(c_pad, ada_w, ada_b.reshape(depth, 1, six_d))
    return out[:, :b, :].reshape(depth, b, 6, 1, d)


def _vec_spec(rows_per_batch_tiles):
    return pl.BlockSpec((1, 1, D_MODEL), lambda i: (i // rows_per_batch_tiles, 0, 0))


def _const_spec(shape):
    nd = len(shape)
    return pl.BlockSpec(shape, lambda i: (0,) * nd)


FFN_TM = 512
FFN_FC = 256


def _ffn_kernel(x_ref, xp_ref, g_ref, sc_ref, sh_ref, gate_ref, wv_ref, wg_ref, cw_ref, cb_ref,
                wd_ref, o_ref, *, tiles_per_batch):
    i = pl.program_id(0)
    x = x_ref[...]
    g, sc, sh = g_ref[...], sc_ref[0], sh_ref[0]
    h = _norm_mod(x, g, sc, sh).astype(BF16)
    hp = _norm_mod(xp_ref[...], g, sc, sh).astype(BF16)
    keep_prev = (i % tiles_per_batch != 0).astype(F32)
    tm = x.shape[0]
    row8 = lax.broadcasted_iota(I32, (SUBLANES, FFN_FC), 0)

    def conv(w_ref, f, col0):
        cs = slice(f * FFN_FC, (f + 1) * FFN_FC)
        up = _bdot(h, w_ref[:, cs])
        upp = _bdot(hp, w_ref[:, cs]) * keep_prev
        p7 = upp[7:8, :]
        p6 = upp[6:7, :]
        up1 = pltpu.roll(up, 1, 0)
        up2 = pltpu.roll(up, 2, 0)
        top1 = jnp.where(row8 == 0, p7, up1[:SUBLANES])
        top2 = jnp.where(row8 == 0, p6, jnp.where(row8 == 1, p7, up2[:SUBLANES]))
        up1 = jnp.concatenate([top1, up1[SUBLANES:]], axis=0)
        up2 = jnp.concatenate([top2, up2[SUBLANES:]], axis=0)
        wsl = slice(col0 + f * FFN_FC, col0 + (f + 1) * FFN_FC)
        cw = cw_ref[:, wsl]
        return up2 * cw[0:1] + up1 * cw[1:2] + up * cw[2:3] + cb_ref[:, wsl]

    acc = jnp.zeros((tm, D_MODEL), F32)
    for f in range(D_FF // FFN_FC):
        val = conv(wv_ref, f, 0)
        gt = conv(wg_ref, f, D_FF)
        act = (gt * _sigmoid(gt) * val).astype(BF16)
        acc = acc + _bdot(act, wd_ref[f * FFN_FC:(f + 1) * FFN_FC, :])
    o_ref[...] = x + gate_ref[0] * acc


def _conv_ffn(x, norm_g, sc, sh, gate, w_val, w_gate, conv_w, conv_b, w_down, seq_len):
    n, d = x.shape
    tm = min(FFN_TM, seq_len)
    tpb = seq_len // tm
    kern = functools.partial(_ffn_kernel, tiles_per_batch=tpb)
    return pl.pallas_call(
        kern,
        grid=(n // tm,),
        in_specs=[
            pl.BlockSpec((tm, d), lambda i: (i, 0)),
            pl.BlockSpec((SUBLANES, d), lambda i: (jnp.maximum(i * (tm // SUBLANES) - 1, 0), 0)),
            _const_spec((1, d)),
            _vec_spec(tpb), _vec_spec(tpb), _vec_spec(tpb),
            _const_spec((d, D_FF)), _const_spec((d, D_FF)),
            _const_spec((3, 2 * D_FF)), _const_spec((1, 2 * D_FF)),
            _const_spec((D_FF, d)),
        ],
        out_specs=pl.BlockSpec((tm, d), lambda i: (i, 0)),
        out_shape=jax.ShapeDtypeStruct((n, d), F32),
        compiler_params=_cparams("parallel"),
        name="conv_ffn",
    )(x, x, norm_g, sc, sh, gate, w_val, w_gate, conv_w, conv_b, w_down)


ROPE_TM = 512


def _rope_kernel(pos_ref, inv_ref, cos_ref, sa_ref, sb_ref):
    ang = pos_ref[...].astype(F32) * inv_ref[...]
    cs = jnp.cos(ang)
    sn = jnp.sin(ang)
    lane = lax.broadcasted_iota(I32, ang.shape, 1)
    low = (lane % HEAD_DIM) < (HEAD_DIM // 2)
    cos_ref[...] = cs
    sa_ref[...] = jnp.where(low, -sn, 0.0)
    sb_ref[...] = jnp.where(low, 0.0, sn)


def _rope_tables(positions):
    n = positions.size
    tm = min(ROPE_TM, n)
    half = HEAD_DIM // 2
    inv = ROPE_THETA ** (-jnp.arange(half, dtype=F32) / half)
    inv = jnp.tile(inv, LANES // half).reshape(1, LANES)
    spec = pl.BlockSpec((tm, LANES), lambda i: (i, 0))
    return pl.pallas_call(
        _rope_kernel,
        grid=(n // tm,),
        in_specs=[pl.BlockSpec((tm, 1), lambda i: (i, 0)), _const_spec((1, LANES))],
        out_specs=[spec, spec, spec],
        out_shape=[jax.ShapeDtypeStruct((n, LANES), F32)] * 3,
        compiler_params=_cparams("parallel"),
        name="rope_tables",
    )(positions.reshape(n, 1), inv)


def _rope(x, cos_t, sin_a, sin_b):
    return (x * cos_t + pltpu.roll(x, LANES - HEAD_DIM // 2, 1) * sin_a
            + pltpu.roll(x, HEAD_DIM // 2, 1) * sin_b)


PROJ_TM = 256


def _attn_proj_kernel(x_ref, g_ref, sc_ref, sh_ref, w_ref, qg_ref, kg_ref, cos_ref, sa_ref, sb_ref,
                      hm_ref, qt_ref, k_ref, vt_ref, qit_ref, ki_ref, wit_ref):
    h = _norm_mod(x_ref[...], g_ref[...], sc_ref[0], sh_ref[0]).astype(BF16)
    cos_t, sin_a, sin_b = cos_ref[...], sa_ref[...], sb_ref[...]
    hm = hm_ref[...]
    n_qk = D_MODEL // LANES

    def group(gi):
        return _bdot(h, w_ref[:, gi * LANES:(gi + 1) * LANES])

    def head_norm(p, gain):
        sq = p * p
        hi = sq.astype(BF16)
        lo = (sq - hi.astype(F32)).astype(BF16)
        ms = _bdot(hi, hm) + _bdot(lo, hm)
        return p * lax.rsqrt(ms + EPS) * gain

    for gi in range(n_qk):
        rows = slice(gi * LANES, (gi + 1) * LANES)
        q = _rope(head_norm(group(gi), qg_ref[...]), cos_t, sin_a, sin_b)
        qt_ref[rows, :] = (q * (HEAD_DIM ** -0.5 * LOG2E)).T.astype(BF16)
        k = _rope(head_norm(group(n_qk + gi), kg_ref[...]), cos_t, sin_a, sin_b)
        k_ref[:, rows] = k.astype(BF16)
        vt_ref[rows, :] = group(2 * n_qk + gi).T.astype(BF16)
    n_qi = IDX_HEADS * IDX_DIM // LANES
    for gi in range(n_qi):
        qi = _rope(group(3 * n_qk + gi), cos_t, sin_a, sin_b)
        qit_ref[gi * LANES:(gi + 1) * LANES, :] = qi.T.astype(BF16)
    tail = group(3 * n_qk + n_qi)
    ki = _rope(tail, cos_t, sin_a, sin_b)
    lane = lax.broadcasted_iota(I32, ki.shape, 1)
    ki_ref[...] = jnp.where(lane < IDX_DIM, ki, pltpu.roll(ki, IDX_DIM, 1)).astype(BF16)
    wi_t = (tail * (IDX_HEADS ** -0.5 * IDX_DIM ** -0.5)).T
    wit_ref[...] = wi_t[IDX_DIM:IDX_DIM + IDX_HEADS, :]


def _attn_proj(x, norm_g, sc, sh, w_in, q_gain, k_gain, cos_t, sin_a, sin_b, seq_len):
    n, d = x.shape
    tm = min(PROJ_TM, seq_len)
    tpb = seq_len // tm
    lane = jnp.arange(LANES)
    hm = ((lane[:, None] // HEAD_DIM) == (lane[None, :] // HEAD_DIM)).astype(BF16) / HEAD_DIM
    qg2 = jnp.tile(q_gain, LANES // HEAD_DIM).reshape(1, LANES)
    kg2 = jnp.tile(k_gain, LANES // HEAD_DIM).reshape(1, LANES)
    row = lambda w: pl.BlockSpec((tm, w), lambda i: (i, 0))
    col = lambda h: pl.BlockSpec((h, tm), lambda i: (0, i))
    n_idx = IDX_HEADS * IDX_DIM
    return pl.pallas_call(
        _attn_proj_kernel,
        grid=(n // tm,),
        in_specs=[
            row(d), _const_spec((1, d)), _vec_spec(tpb), _vec_spec(tpb),
            _const_spec((d, IN_COLS_PAD)), _const_spec((1, LANES)), _const_spec((1, LANES)),
            row(LANES), row(LANES), row(LANES), _const_spec((LANES, LANES)),
        ],
        out_specs=[col(d), row(d), col(d), col(n_idx), row(LANES), col(IDX_HEADS)],
        out_shape=[
            jax.ShapeDtypeStruct((d, n), BF16),
            jax.ShapeDtypeStruct((n, d), BF16),
            jax.ShapeDtypeStruct((d, n), BF16),
            jax.ShapeDtypeStruct((n_idx, n), BF16),
            jax.ShapeDtypeStruct((n, LANES), BF16),
            jax.ShapeDtypeStruct((IDX_HEADS, n), F32),
        ],
        compiler_params=_cparams("parallel"),
        name="attn_proj",
    )(x, norm_g, sc, sh, w_in, qg2, kg2, cos_t, sin_a, sin_b, hm)


ATT_QB = 256
ATT_TS = 512
ATT_RS = 64
ATT_TA = 2048
LOG2E = math.log2(math.e)
INT_MIN = -(2 ** 31)


def _sort_key(s):
    bits = pltpu.bitcast(s + 0.0, I32)
    return bits ^ ((bits >> 31) & 0x7FFFFFFF)


def _attn_kernel(qt_ref, k_ref, vt_ref, qit_ref, ki_ref, wit_ref, o_ref, key_ref, qm_ref,
                 *, seq_len, topk):
    qb = pl.program_id(1)
    hp = pl.program_id(2)
    nq, ts, rs, ta = qt_ref.shape[1], ATT_TS, ATT_RS, min(ATT_TA, seq_len)
    q0 = qb * nq
    n_tiles = (q0 + nq + ts - 1) // ts
    p_sel = n_tiles * ts
    n_out = seq_len - p_sel
    sub = lax.broadcasted_iota(I32, (rs, nq), 0)
    key_neg = _sort_key(jnp.full((rs, nq), NEG, F32))

    @pl.when(hp == 0)
    def _select():
        row_head = lax.broadcasted_iota(I32, (LANES, nq), 0) // IDX_DIM
        for h in range(IDX_HEADS):
            pair = qit_ref[(h // 2) * LANES:(h // 2 + 1) * LANES, :]
            qm_ref[h] = jnp.where(row_head == (h % 2), pair, jnp.zeros_like(pair))
        q_chunk = (q0 + lax.broadcasted_iota(I32, (ts, nq), 1)) // CHUNK
        k_row = lax.broadcasted_iota(I32, (ts, nq), 0)

        def score_tile(j, carry):
            r0 = pl.multiple_of(j * ts, ts)
            ki_t = ki_ref[pl.ds(r0, ts), :]
            s = jnp.zeros((ts, nq), F32)
            for h in range(IDX_HEADS):
                s = s + wit_ref[h:h + 1, :] * jnp.maximum(_bdot(ki_t, qm_ref[h]), 0.0)
            adm = ((r0 + k_row) // CHUNK) <= q_chunk
            s = jnp.where(adm, s, NEG)
            key_ref[pl.ds(r0, ts), :] = _sort_key(s)
            return carry

        lax.fori_loop(0, n_tiles, score_tile, 0)

        def count(ind_fn):
            def body(j, cnt):
                r0 = pl.multiple_of(j * ts, ts)
                for u in range(ts // rs):
                    ks = key_ref[pl.ds(r0 + u * rs, rs), :]
                    cnt = cnt + ind_fn(ks, r0 + u * rs)
                return cnt
            cnt = lax.fori_loop(0, n_tiles, body, jnp.zeros((rs, nq), I32))
            tot = jnp.sum(cnt.astype(F32), axis=0, keepdims=True).astype(I32)
            return jnp.broadcast_to(tot, (rs, nq))

        def bit_cond(st):
            b, _, cnt_t = st
            return (b < 32) & (jnp.max(cnt_t.astype(F32)) > float(topk))

        def bit_step(st):
            b, t, cnt_t = st
            bit = jnp.left_shift(jnp.int32(1), 31 - b)
            cand = jnp.where(b == 0, jnp.zeros_like(t), t | bit)
            cnt = (count(lambda ks, c0: jnp.where(ks >= cand, 1, 0))
                   + jnp.where(cand <= key_neg, n_out, 0))
            take = cnt >= topk
            return b + 1, jnp.where(take, cand, t), jnp.where(take, cnt, cnt_t)

        _, t, cnt_t = lax.while_loop(
            bit_cond, bit_step,
            (jnp.int32(0), jnp.full((rs, nq), INT_MIN, I32), jnp.full((rs, nq), seq_len, I32)))

        def tie_cut():
            n_gt = (count(lambda ks, c0: jnp.where(ks > t, 1, 0))
                    + jnp.where(t < key_neg, n_out, 0))
            n_eq = count(lambda ks, c0: jnp.where(ks == t, 1, 0))
            need = topk - n_gt
            real = jnp.where(t > key_neg, 1, 0)
            n_amb = jnp.sum((real * jnp.where(n_eq > need, 1, 0)).astype(F32))

            def search():
                def idx_step(b, c):
                    cand = c | jnp.left_shift(jnp.int32(1), 14 - b)
                    f = count(lambda ks, r0: jnp.where(
                        ks == t, jnp.where((r0 + sub) < cand, 1, 0), 0))
                    return jnp.where(f <= need, cand, c)
                return real * lax.fori_loop(0, 15, idx_step, jnp.zeros((rs, nq), I32))

            return lax.cond(n_amb > 0.0, search, lambda: real * seq_len)

        exact_sets = jnp.max(cnt_t.astype(F32)) <= float(topk)
        cut = lax.cond(exact_sets, lambda: jnp.full((rs, nq), seq_len, I32), tie_cut)

        def bias_tile(j, carry):
            r0 = pl.multiple_of(j * ts, ts)
            for u in range(ts // rs):
                ru = r0 + u * rs
                ks = key_ref[pl.ds(ru, rs), :]
                tie_ok = jnp.where((ru + sub) < cut, 0.0, NEG)
                bias = jnp.where(ks > t, 0.0, jnp.where(ks == t, tie_ok, NEG))
                key_ref[pl.ds(ru, rs), :] = pltpu.bitcast(bias, I32)
            return carry

        lax.fori_loop(0, n_tiles, bias_tile, 0)

    qt = qt_ref[...]
    row_head = lax.broadcasted_iota(I32, (LANES, nq), 0) // HEAD_DIM
    qh = [jnp.where(row_head == hh, qt, jnp.zeros_like(qt)) for hh in range(2)]

    def att_tile(r0, size, carry):
        k_t = k_ref[pl.ds(r0, size), :]
        bias = pltpu.bitcast(key_ref[pl.ds(r0, size), :], F32)
        new = []
        logits = [_bdot(k_t, qh[hh]) + bias for hh in range(2)]
        for hh in range(2):
            m, l, acc = carry[hh]
            s = logits[hh]
            m_new = jnp.maximum(m, jnp.max(s, axis=0, keepdims=True))
            alpha = jnp.exp2(m - m_new)
            p = jnp.exp2(s - m_new)
            l = alpha * l + jnp.sum(p, axis=0, keepdims=True)
            vt = vt_ref[hh * HEAD_DIM:(hh + 1) * HEAD_DIM, pl.ds(r0, size)]
            acc = alpha * acc + _bdot(vt, p.astype(BF16))
            new.append((m_new, l, acc))
        return tuple(new)

    head_init = (jnp.full((1, nq), NEG, F32), jnp.zeros((1, nq), F32), jnp.zeros((HEAD_DIM, nq), F32))
    carry = lax.fori_loop(0, p_sel // ta,
                          lambda j, c: att_tile(pl.multiple_of(j * ta, ta), ta, c),
                          (head_init, head_init))
    off = (p_sel // ta) * ta
    size = ta // 2
    while size >= ts:
        take = ((p_sel - off) & size) != 0
        carry = lax.cond(take, functools.partial(att_tile, pl.multiple_of(off, ts), size),
                         lambda c: c, carry)
        off = off + jnp.where(take, size, 0)
        size //= 2
    for hh in range(2):
        _, l, acc = carry[hh]
        o_ref[hh * HEAD_DIM:(hh + 1) * HEAD_DIM, :] = (acc / l).astype(BF16)


def _dsa_attention(qt, k, vt, qit, ki, wit, batch, seq_len):
    d, n = qt.shape
    nq = min(ATT_QB, seq_len)
    qpb = seq_len // nq
    n_idx = IDX_HEADS * IDX_DIM
    topk = min(TOPK_MAX, seq_len // 4)
    kern = functools.partial(_attn_kernel, seq_len=seq_len, topk=topk)
    return pl.pallas_call(
        kern,
        grid=(batch, qpb, d // LANES),
        in_specs=[
            pl.BlockSpec((LANES, nq), lambda b, i, h: (h, b * qpb + i)),
            pl.BlockSpec((seq_len, LANES), lambda b, i, h: (b, h)),
            pl.BlockSpec((LANES, seq_len), lambda b, i, h: (h, b)),
            pl.BlockSpec((n_idx, nq), lambda b, i, h: (0, b * qpb + i)),
            pl.BlockSpec((seq_len, LANES), lambda b, i, h: (b, 0)),
            pl.BlockSpec((IDX_HEADS, nq), lambda b, i, h: (0, b * qpb + i)),
        ],
        out_specs=pl.BlockSpec((LANES, nq), lambda b, i, h: (h, b * qpb + i)),
        out_shape=jax.ShapeDtypeStruct((d, n), BF16),
        scratch_shapes=[pltpu.VMEM((seq_len, nq), I32), pltpu.VMEM((IDX_HEADS, LANES, nq), BF16)],
        compiler_params=_cparams("parallel", "parallel", "arbitrary"),
        name="dsa_attention",
    )(qt, k, vt, qit, ki, wit)


OUT_TM = 512


def _out_proj_kernel(x_ref, at_ref, w_ref, gate_ref, o_ref):
    y = lax.dot_general(at_ref[...], w_ref[...], (((0,), (0,)), ((), ())),
                        preferred_element_type=F32)
    o_ref[...] = x_ref[...] + gate_ref[0] * y


def _out_proj(x, a_t, w, gate, seq_len):
    n, d = x.shape
    tm = min(OUT_TM, seq_len)
    tpb = seq_len // tm
    row = pl.BlockSpec((tm, d), lambda i: (i, 0))
    return pl.pallas_call(
        _out_proj_kernel,
        grid=(n // tm,),
        in_specs=[row, pl.BlockSpec((d, tm), lambda i: (0, i)), _const_spec((d, d)), _vec_spec(tpb)],
        out_specs=row,
        out_shape=jax.ShapeDtypeStruct((n, d), F32),
        compiler_params=_cparams("parallel"),
        name="attn_out_proj",
    )(x, a_t, w, gate)


NORMT_TM = 512


def _norm_t_kernel(x_ref, g_ref, sc_ref, sh_ref, o_ref):
    o_ref[...] = _norm_mod(x_ref[...], g_ref[...], sc_ref[0], sh_ref[0]).T


def _norm_transposed(x, norm_g, sc, sh, seq_len):
    n, d = x.shape
    tm = min(NORMT_TM, seq_len)
    tpb = seq_len // tm
    return pl.pallas_call(
        _norm_t_kernel,
        grid=(n // tm,),
        in_specs=[pl.BlockSpec((tm, d), lambda i: (i, 0)), _const_spec((1, d)),
                  _vec_spec(tpb), _vec_spec(tpb)],
        out_specs=pl.BlockSpec((d, tm), lambda i: (0, i)),
        out_shape=jax.ShapeDtypeStruct((d, n), F32),
        compiler_params=_cparams("parallel"),
        name="ssm_norm_t",
    )(x, norm_g, sc, sh)


def _ssm_prep_kernel(are_ref, aim_ref, ldt_ref, brt_ref, bit_ref, cr_ref, ci_ref, crt_ref, cit_ref,
                     toep_ref, bst_ref, cout_ref, lam_ref, kt_ref):
    t_len, p, sg = SSM_T, SSM_STATE, SSM_GROUP
    a_re, a_im = are_ref[0], aim_ref[0]
    dt = jnp.exp(ldt_ref[0])
    ar, th = a_re * dt, a_im * dt
    decay = jnp.exp(ar)
    ab_re, ab_im = decay * jnp.cos(th), decay * jnp.sin(th)
    den = a_re * a_re + a_im * a_im
    nr, ni = ab_re - 1.0, ab_im
    coef_re = (nr * a_re + ni * a_im) / den
    coef_im = (ni * a_re - nr * a_im) / den
    brt, bit = brt_ref[0], bit_ref[0]
    bbr = coef_re * brt - coef_im * bit
    bbi = coef_re * bit + coef_im * brt
    cr, ci = cr_ref[0], ci_ref[0]
    crt, cit = crt_ref[0], cit_ref[0]

    def powers(nn):
        mag = jnp.exp(nn * ar)
        return mag * jnp.cos(nn * th), mag * jnp.sin(nn * th)

    n_iota = lax.broadcasted_iota(I32, (t_len, p), 0).astype(F32)
    l_re, l_im = powers(n_iota)
    l1_re, l1_im = powers(n_iota + 1.0)
    lr_re, lr_im = powers((t_len - 1.0) - n_iota)
    lcat_t = jnp.concatenate([l_re, l_im], axis=1).T
    l1cat_t = jnp.concatenate([l1_re, l1_im], axis=1).T
    l1_re_t, l1_im_t = l1cat_t[:p], l1cat_t[p:]

    mt = []
    for c in range(sg):
        m_re = cr[c:c + 1] * bbr - ci[c:c + 1] * bbi
        m_im = -(cr[c:c + 1] * bbi + ci[c:c + 1] * bbr)
        mt.append(jnp.concatenate([m_re, m_im], axis=1))
    mt = jnp.concatenate(mt, axis=0)
    kt_ref[...] = jnp.dot(mt, lcat_t, preferred_element_type=F32, precision=lax.Precision.HIGHEST)

    tau = lax.broadcasted_iota(I32, (t_len, t_len), 0)
    tt = lax.broadcasted_iota(I32, (t_len, t_len), 1)
    causal = tt >= tau

    def toep_tile(r, carry):
        c = r // sg
        cp = r % sg
        krow = jnp.broadcast_to(kt_ref[pl.ds(r, 1), :], (t_len, t_len))
        tile = pltpu.roll(krow, 0, 1, stride=1, stride_axis=0)
        tile = jnp.where(causal, tile, 0.0).astype(BF16)
        toep_ref[0, pl.ds(pl.multiple_of(cp * t_len, t_len), t_len),
                 pl.ds(pl.multiple_of(c * t_len, t_len), t_len)] = tile
        return carry

    lax.fori_loop(0, sg * sg, toep_tile, 0)

    for c in range(sg):
        bst_ref[0, c * t_len:(c + 1) * t_len, :] = jnp.concatenate(
            [lr_re * bbr[c:c + 1] - lr_im * bbi[c:c + 1],
             lr_re * bbi[c:c + 1] + lr_im * bbr[c:c + 1]], axis=1).astype(BF16)
        ccr, cci = crt[:, c:c + 1], cit[:, c:c + 1]
        cout_ref[0, :, c * t_len:(c + 1) * t_len] = jnp.concatenate(
            [ccr * l1_re_t - cci * l1_im_t, -ccr * l1_im_t - cci * l1_re_t], axis=0).astype(BF16)

    lt_re, lt_im = powers(jnp.full((SUBLANES, p), float(t_len), F32))
    lam_ref[0] = jnp.concatenate([lt_re, lt_im], axis=1)


def _ssm_prep(a_re, a_im, log_dt, b_re, b_im, c_re, c_im):
    g, p, sg, t_len = N_GROUPS, SSM_STATE, SSM_GROUP, SSM_T
    vec = lambda a: a.reshape(g, 1, -1)
    spec3 = lambda s: pl.BlockSpec((1,) + s, lambda i: (i, 0, 0))
    return pl.pallas_call(
        _ssm_prep_kernel,
        grid=(g,),
        in_specs=[spec3((1, p)), spec3((1, p)), spec3((1, 1)), spec3((sg, p)), spec3((sg, p)),
                  spec3((sg, p)), spec3((sg, p)), spec3((p, sg)), spec3((p, sg))],
        out_specs=[spec3((sg * t_len, sg * t_len)), spec3((sg * t_len, 2 * p)),
                   spec3((2 * p, sg * t_len)), spec3((SUBLANES, 2 * p))],
        out_shape=[
            jax.ShapeDtypeStruct((g, sg * t_len, sg * t_len), BF16),
            jax.ShapeDtypeStruct((g, sg * t_len, 2 * p), BF16),
            jax.ShapeDtypeStruct((g, 2 * p, sg * t_len), BF16),
            jax.ShapeDtypeStruct((g, SUBLANES, 2 * p), F32),
        ],
        scratch_shapes=[pltpu.VMEM((sg * sg, t_len), F32)],
        compiler_params=_cparams("parallel"),
        name="ssm_prep",
    )(vec(a_re), vec(a_im), vec(log_dt), b_re.transpose(0, 2, 1), b_im.transpose(0, 2, 1),
      c_re, c_im, c_re.transpose(0, 2, 1), c_im.transpose(0, 2, 1))


def _gelu_tanh(x):
    return 0.5 * x * (1.0 + jnp.tanh(math.sqrt(2.0 / math.pi) * (x + 0.044715 * (x * x * x))))


def _ssm_kernel(u_ref, toep_ref, bst_ref, cout_ref, lam_ref, d_ref, o_ref, *, chunks_per_batch):
    sg, t_len, p = SSM_GROUP, SSM_T, SSM_STATE
    nc = u_ref.shape[1]
    lhs = jnp.concatenate([u_ref[c].astype(BF16) for c in range(sg)], axis=1)
    state = _bdot(lhs, bst_ref[0])
    row = lax.broadcasted_iota(I32, (nc, 2 * p), 0) % chunks_per_batch
    lane = lax.broadcasted_iota(I32, (1, 2 * p), 1)
    lam = lam_ref[0]
    a_re = lam[0:1, :p]
    a_im = lam[0:1, p:]

    def cmul_rows(a_re, a_im, x):
        a1 = jnp.concatenate([a_re, a_re], axis=1)
        a2 = jnp.concatenate([-a_im, a_im], axis=1)
        return a1 * x + a2 * pltpu.roll(x, p, 1)

    k = 1
    while k < chunks_per_batch:
        shifted = jnp.where(row >= k, pltpu.roll(state, k, 0), 0.0)
        state = state + cmul_rows(a_re, a_im, shifted)
        a_re, a_im = a_re * a_re - a_im * a_im, 2.0 * a_re * a_im
        k *= 2
    carry = jnp.where(row >= 1, pltpu.roll(state, 1, 0), 0.0)
    del lane
    y = _bdot(lhs, toep_ref[0]) + _bdot(carry.astype(BF16), cout_ref[0])
    for c in range(sg):
        yc = y[:, c * t_len:(c + 1) * t_len] + d_ref[0, c:c + 1, :] * u_ref[c]
        o_ref[c] = _gelu_tanh(yc).astype(BF16)


def _ssm_apply(u_t, toep, bst, cout, lam, d_skip, seq_len):
    d, n = u_t.shape
    g, sg, t_len, p = N_GROUPS, SSM_GROUP, SSM_T, SSM_STATE
    nc = n // t_len
    u3 = u_t.reshape(d, nc, t_len)
    d3 = jnp.broadcast_to(d_skip.reshape(g, sg, 1), (g, sg, t_len))
    kern = functools.partial(_ssm_kernel, chunks_per_batch=seq_len // t_len)
    spec3 = lambda s: pl.BlockSpec((1,) + s, lambda i: (i, 0, 0))
    blk = pl.BlockSpec((sg, nc, t_len), lambda i: (i, 0, 0))
    out = pl.pallas_call(
        kern,
        grid=(g,),
        in_specs=[blk, spec3((sg * t_len, sg * t_len)), spec3((sg * t_len, 2 * p)),
                  spec3((2 * p, sg * t_len)), spec3((SUBLANES, 2 * p)), spec3((sg, t_len))],
        out_specs=blk,
        out_shape=jax.ShapeDtypeStruct((d, nc, t_len), BF16),
        compiler_params=_cparams("parallel"),
        name="ssm_apply",
    )(u3, toep, bst, cout, lam, d3)
    return out.reshape(d, n)


GLU_TM = 512


def _glu_kernel(x_ref, gt_ref, w_ref, gate_ref, o_ref):
    z = lax.dot_general(gt_ref[...], w_ref[...], (((0,), (0,)), ((), ())),
                        preferred_element_type=F32)
    d = x_ref.shape[1]
    o_ref[...] = x_ref[...] + gate_ref[0] * (z[:, :d] * _sigmoid(z[:, d:]))


def _glu_residual(x, g_t, w_glu, gate, seq_len):
    n, d = x.shape
    tm = min(GLU_TM, seq_len)
    tpb = seq_len // tm
    row = pl.BlockSpec((tm, d), lambda i: (i, 0))
    return pl.pallas_call(
        _glu_kernel,
        grid=(n // tm,),
        in_specs=[row, pl.BlockSpec((d, tm), lambda i: (0, i)), _const_spec((d, 2 * d)),
                  _vec_spec(tpb)],
        out_specs=row,
        out_shape=jax.ShapeDtypeStruct((n, d), F32),
        compiler_params=_cparams("parallel"),
        name="ssm_glu",
    )(x, g_t, w_glu, gate)


def kernel(x, c, positions, ada_w, ada_b, norm_mix, norm_ffn, attn_w_in, attn_q_gain, attn_k_gain,
           attn_w_out, ssm_a_re, ssm_a_im, ssm_log_dt, ssm_b_re, ssm_b_im, ssm_c_re, ssm_c_im,
           ssm_d, ssm_w_glu, ffn_w_up, ffn_conv_w, ffn_conv_b, ffn_w_down):
    batch, seq_len, d = x.shape
    depth = ada_w.shape[0]
    n = batch * seq_len
    xs = x.reshape(n, d)
    mod = _modulation(c, ada_w, ada_b)
    cos_t, sin_a, sin_b = _rope_tables(positions)
    for i in range(depth):
        sh_m, sc_m, g_m, sh_f, sc_f, g_f = [mod[i, :, k] for k in range(6)]
        j = i // 2
        if i % 2 == 0:
            w_in = jnp.pad(attn_w_in[j], ((0, 0), (0, IN_COLS_PAD - IN_COLS))).astype(BF16)
            qt, k, vt, qit, ki, wit = _attn_proj(
                xs, norm_mix[i].reshape(1, d), sc_m, sh_m, w_in, attn_q_gain[j], attn_k_gain[j],
                cos_t, sin_a, sin_b, seq_len)
            o = _dsa_attention(qt, k, vt, qit, ki, wit, batch, seq_len)
            xs = _out_proj(xs, o, attn_w_out[j].astype(BF16), g_m, seq_len)
        else:
            u_t = _norm_transposed(xs, norm_mix[i].reshape(1, d), sc_m, sh_m, seq_len)
            toep, bst, cout, lam = _ssm_prep(ssm_a_re[j], ssm_a_im[j], ssm_log_dt[j], ssm_b_re[j],
                                             ssm_b_im[j], ssm_c_re[j], ssm_c_im[j])
            g_t = _ssm_apply(u_t, toep, bst, cout, lam, ssm_d[j], seq_len)
            xs = _glu_residual(xs, g_t, ssm_w_glu[j].astype(BF16), g_m, seq_len)
        w_up = ffn_w_up[i].astype(BF16)
        xs = _conv_ffn(xs, norm_ffn[i].reshape(1, d), sc_f, sh_f, g_f, w_up[:, :D_FF], w_up[:, D_FF:],
                       ffn_conv_w[i], ffn_conv_b[i].reshape(1, -1), ffn_w_down[i].astype(BF16),
                       seq_len)
    return xs.reshape(batch, seq_len, d)
```

```python
import functools
import math

import jax
import jax.numpy as jnp
from jax import lax
from jax.experimental import pallas as pl
from jax.experimental.pallas import tpu as pltpu

F32 = jnp.float32
BF16 = jnp.bfloat16
I32 = jnp.int32

LANES = 128
SUBLANES = 8
VMEM_LIMIT = 56 << 20

D_MODEL = 1024
N_HEADS = 16
HEAD_DIM = 64
IDX_HEADS = 8
IDX_DIM = 64
CHUNK = 64
TOPK_MAX = 256
ROPE_THETA = 10000.0
SSM_GROUP = 16
N_GROUPS = D_MODEL // SSM_GROUP
SSM_STATE = 64
D_FF = 2816
EPS = 1e-6
NEG = -1e30
IN_COLS = 3 * D_MODEL + IDX_HEADS * IDX_DIM + IDX_DIM + IDX_HEADS
IN_COLS_PAD = 29 * LANES
SSM_T = 128


def _cparams(*sem):
    return pltpu.CompilerParams(dimension_semantics=sem, vmem_limit_bytes=VMEM_LIMIT)


def _sigmoid(x):
    return 1.0 / (1.0 + jnp.exp(-x))


def _norm_mod(x, g, sc, sh):
    ms = jnp.mean(x * x, axis=-1, keepdims=True)
    return (x * lax.rsqrt(ms + EPS) * g) * (1.0 + sc) + sh


def _bdot(a, b):
    return jnp.dot(a, b, preferred_element_type=F32)


def _mod_kernel(c_ref, w_ref, b_ref, o_ref):
    c = c_ref[...]
    cond = c * _sigmoid(c)
    o_ref[0] = _bdot(cond.astype(BF16), w_ref[0].astype(BF16)) + b_ref[0]


def _modulation(c, ada_w, ada_b):
    depth, d, six_d = ada_w.shape
    b = c.shape[0]
    c_pad = jnp.zeros((SUBLANES, d), F32).at[:b].set(c)
    out = pl.pallas_call(
        _mod_kernel,
        grid=(depth, six_d // d),
        in_specs=[
            pl.BlockSpec((SUBLANES, d), lambda i, j: (0, 0)),
            pl.BlockSpec((1, d, d), lambda i, j: (i, 0, j)),
            pl.BlockSpec((1, 1, d), lambda i, j: (i, 0, j)),
        ],
        out_specs=pl.BlockSpec((1, SUBLANES, d), lambda i, j: (i, 0, j)),
        out_shape=jax.ShapeDtypeStruct((depth, SUBLANES, six_d), F32),
        compiler_params=_cparams("parallel", "parallel"),
        name="adaln_mod",
    )(c_pad, ada_w, ada_b.reshape(depth, 1, six_d))
    return out[:, :b, :].reshape(depth, b, 6, 1, d)


def _vec_spec(rows_per_batch_tiles):
    return pl.BlockSpec((1, 1, D_MODEL), lambda i: (i // rows_per_batch_tiles, 0, 0))


def _const_spec(shape):
    nd = len(shape)
    return pl.BlockSpec(shape, lambda i: (0,) * nd)


FFN_TM = 512
FFN_FC = 256


def _ffn_kernel(x_ref, xp_ref, g_ref, sc_ref, sh_ref, gate_ref, wv_ref, wg_ref, cw_ref, cb_ref,
                wd_ref, o_ref, *, tiles_per_batch):
    i = pl.program_id(0)
    x = x_ref[...]
    g, sc, sh = g_ref[...], sc_ref[0], sh_ref[0]
    h = _norm_mod(x, g, sc, sh).astype(BF16)
    hp = _norm_mod(xp_ref[...], g, sc, sh).astype(BF16)
    keep_prev = (i % tiles_per_batch != 0).astype(F32)
    tm = x.shape[0]
    row8 = lax.broadcasted_iota(I32, (SUBLANES, FFN_FC), 0)

    def up_proj(w_ref, f):
        cs = slice(f * FFN_FC, (f + 1) * FFN_FC)
        return _bdot(h, w_ref[:, cs]), _bdot(hp, w_ref[:, cs]) * keep_prev

    def conv(ups, f, col0):
        up, upp = ups
        p7 = upp[7:8, :]
        p6 = upp[6:7, :]
        up1 = pltpu.roll(up, 1, 0)
        up2 = pltpu.roll(up, 2, 0)
        top1 = jnp.where(row8 == 0, p7, up1[:SUBLANES])
        top2 = jnp.where(row8 == 0, p6, jnp.where(row8 == 1, p7, up2[:SUBLANES]))
        up1 = jnp.concatenate([top1, up1[SUBLANES:]], axis=0)
        up2 = jnp.concatenate([top2, up2[SUBLANES:]], axis=0)
        wsl = slice(col0 + f * FFN_FC, col0 + (f + 1) * FFN_FC)
        cw = cw_ref[:, wsl]
        return up2 * cw[0:1] + up1 * cw[1:2] + up * cw[2:3] + cb_ref[:, wsl]

    acc = jnp.zeros((tm, D_MODEL), F32)
    n_f = D_FF // FFN_FC
    ups = (up_proj(wv_ref, 0), up_proj(wg_ref, 0))
    for f in range(n_f):
        cur = ups
        if f + 1 < n_f:
            ups = (up_proj(wv_ref, f + 1), up_proj(wg_ref, f + 1))
        val = conv(cur[0], f, 0)
        gt = conv(cur[1], f, D_FF)
        act = (gt * _sigmoid(gt) * val).astype(BF16)
        acc = acc + _bdot(act, wd_ref[f * FFN_FC:(f + 1) * FFN_FC, :])
    o_ref[...] = x + gate_ref[0] * acc


def _conv_ffn(x, norm_g, sc, sh, gate, w_val, w_gate, conv_w, conv_b, w_down, seq_len):
    n, d = x.shape
    tm = min(FFN_TM, seq_len)
    tpb = seq_len // tm
    kern = functools.partial(_ffn_kernel, tiles_per_batch=tpb)
    return pl.pallas_call(
        kern,
        grid=(n // tm,),
        in_specs=[
            pl.BlockSpec((tm, d), lambda i: (i, 0)),
            pl.BlockSpec((SUBLANES, d), lambda i: (jnp.maximum(i * (tm // SUBLANES) - 1, 0), 0)),
            _const_spec((1, d)),
            _vec_spec(tpb), _vec_spec(tpb), _vec_spec(tpb),
            _const_spec((d, D_FF)), _const_spec((d, D_FF)),
            _const_spec((3, 2 * D_FF)), _const_spec((1, 2 * D_FF)),
            _const_spec((D_FF, d)),
        ],
        out_specs=pl.BlockSpec((tm, d), lambda i: (i, 0)),
        out_shape=jax.ShapeDtypeStruct((n, d), F32),
        compiler_params=_cparams("parallel"),
        name="conv_ffn",
    )(x, x, norm_g, sc, sh, gate, w_val, w_gate, conv_w, conv_b, w_down)


ROPE_TM = 512


def _rope_kernel(pos_ref, inv_ref, cos_ref, sa_ref, sb_ref):
    ang = pos_ref[...].astype(F32) * inv_ref[...]
    cs = jnp.cos(ang)
    sn = jnp.sin(ang)
    lane = lax.broadcasted_iota(I32, ang.shape, 1)
    low = (lane % HEAD_DIM) < (HEAD_DIM // 2)
    cos_ref[...] = cs
    sa_ref[...] = jnp.where(low, -sn, 0.0)
    sb_ref[...] = jnp.where(low, 0.0, sn)


def _rope_tables(positions):
    n = positions.size
    tm = min(ROPE_TM, n)
    half = HEAD_DIM // 2
    inv = ROPE_THETA ** (-jnp.arange(half, dtype=F32) / half)
    inv = jnp.tile(inv, LANES // half).reshape(1, LANES)
    spec = pl.BlockSpec((tm, LANES), lambda i: (i, 0))
    return pl.pallas_call(
        _rope_kernel,
        grid=(n // tm,),
        in_specs=[pl.BlockSpec((tm, 1), lambda i: (i, 0)), _const_spec((1, LANES))],
        out_specs=[spec, spec, spec],
        out_shape=[jax.ShapeDtypeStruct((n, LANES), F32)] * 3,
        compiler_params=_cparams("parallel"),
        name="rope_tables",
    )(positions.reshape(n, 1), inv)


def _rope(x, cos_t, sin_a, sin_b):
    return (x * cos_t + pltpu.roll(x, LANES - HEAD_DIM // 2, 1) * sin_a
            + pltpu.roll(x, HEAD_DIM // 2, 1) * sin_b)


PROJ_TM = 256


def _attn_proj_kernel(x_ref, g_ref, sc_ref, sh_ref, w_ref, qg_ref, kg_ref, cos_ref, sa_ref, sb_ref,
                      hm_ref, qt_ref, k_ref, vt_ref, qit_ref, ki_ref, wit_ref):
    h = _norm_mod(x_ref[...], g_ref[...], sc_ref[0], sh_ref[0]).astype(BF16)
    cos_t, sin_a, sin_b = cos_ref[...], sa_ref[...], sb_ref[...]
    hm = hm_ref[...]
    n_qk = D_MODEL // LANES

    def group(gi):
        return _bdot(h, w_ref[:, gi * LANES:(gi + 1) * LANES])

    def head_norm(p, gain):
        sq = p * p
        hi = sq.astype(BF16)
        lo = (sq - hi.astype(F32)).astype(BF16)
        ms = _bdot(hi, hm) + _bdot(lo, hm)
        return p * lax.rsqrt(ms + EPS) * gain

    for gi in range(n_qk):
        rows = slice(gi * LANES, (gi + 1) * LANES)
        q = _rope(head_norm(group(gi), qg_ref[...]), cos_t, sin_a, sin_b)
        qt_ref[rows, :] = (q * (HEAD_DIM ** -0.5 * LOG2E)).T.astype(BF16)
        k = _rope(head_norm(group(n_qk + gi), kg_ref[...]), cos_t, sin_a, sin_b)
        k_ref[:, rows] = k.astype(BF16)
        vt_ref[rows, :] = group(2 * n_qk + gi).T.astype(BF16)
    n_qi = IDX_HEADS * IDX_DIM // LANES
    for gi in range(n_qi):
        qi = _rope(group(3 * n_qk + gi), cos_t, sin_a, sin_b)
        qit_ref[gi * LANES:(gi + 1) * LANES, :] = qi.T.astype(BF16)
    tail = group(3 * n_qk + n_qi)
    ki = _rope(tail, cos_t, sin_a, sin_b)
    lane = lax.broadcasted_iota(I32, ki.shape, 1)
    ki_ref[...] = jnp.where(lane < IDX_DIM, ki, pltpu.roll(ki, IDX_DIM, 1)).astype(BF16)
    wi_t = (tail * (IDX_HEADS ** -0.5 * IDX_DIM ** -0.5)).T
    wit_ref[...] = wi_t[IDX_DIM:IDX_DIM + IDX_HEADS, :]


def _attn_proj(x, norm_g, sc, sh, w_in, q_gain, k_gain, cos_t, sin_a, sin_b, seq_len):
    n, d = x.shape
    tm = min(PROJ_TM, seq_len)
    tpb = seq_len // tm
    lane = jnp.arange(LANES)
    hm = ((lane[:, None] // HEAD_DIM) == (lane[None, :] // HEAD_DIM)).astype(BF16) / HEAD_DIM
    qg2 = jnp.tile(q_gain, LANES // HEAD_DIM).reshape(1, LANES)
    kg2 = jnp.tile(k_gain, LANES // HEAD_DIM).reshape(1, LANES)
    row = lambda w: pl.BlockSpec((tm, w), lambda i: (i, 0))
    col = lambda h: pl.BlockSpec((h, tm), lambda i: (0, i))
    n_idx = IDX_HEADS * IDX_DIM
    return pl.pallas_call(
        _attn_proj_kernel,
        grid=(n // tm,),
        in_specs=[
            row(d), _const_spec((1, d)), _vec_spec(tpb), _vec_spec(tpb),
            _const_spec((d, IN_COLS_PAD)), _const_spec((1, LANES)), _const_spec((1, LANES)),
            row(LANES), row(LANES), row(LANES), _const_spec((LANES, LANES)),
        ],
        out_specs=[col(d), row(d), col(d), col(n_idx), row(LANES), col(IDX_HEADS)],
        out_shape=[
            jax.ShapeDtypeStruct((d, n), BF16),
            jax.ShapeDtypeStruct((n, d), BF16),
            jax.ShapeDtypeStruct((d, n), BF16),
            jax.ShapeDtypeStruct((n_idx, n), BF16),
            jax.ShapeDtypeStruct((n, LANES), BF16),
            jax.ShapeDtypeStruct((IDX_HEADS, n), F32),
        ],
        compiler_params=_cparams("parallel"),
        name="attn_proj",
    )(x, norm_g, sc, sh, w_in, qg2, kg2, cos_t, sin_a, sin_b, hm)


ATT_QB = 256
ATT_TS = 512
ATT_RS = 64
ATT_SUM_ROWS = 16
LOG2E = math.log2(math.e)
INT_MIN = -(2 ** 31)


def _sort_key(s):
    bits = pltpu.bitcast(s + 0.0, I32)
    return bits ^ ((bits >> 31) & 0x7FFFFFFF)


def _attn_kernel(qt_ref, k_ref, vt_ref, qit_ref, ki_ref, wit_ref, o_ref, key_ref, qm_ref,
                 sa_ref, sb_ref, *, seq_len, topk):
    qb = pl.program_id(1)
    hp = pl.program_id(2)
    nq, ts, rs = qt_ref.shape[1], ATT_TS, ATT_RS
    q0 = qb * nq
    n_tiles = 2 * ((q0 + nq + 2 * ts - 1) // (2 * ts))
    n_out = seq_len - n_tiles * ts
    sub = lax.broadcasted_iota(I32, (rs, nq), 0)
    key_neg = _sort_key(jnp.full((rs, nq), NEG, F32))

    @pl.when(hp == 0)
    def _select():
        row_head = lax.broadcasted_iota(I32, (LANES, nq), 0) // IDX_DIM
        for h in range(IDX_HEADS):
            pair = qit_ref[(h // 2) * LANES:(h // 2 + 1) * LANES, :]
            qm_ref[h] = jnp.where(row_head == (h % 2), pair, jnp.zeros_like(pair))
        q_chunk = (q0 + lax.broadcasted_iota(I32, (ts, nq), 1)) // CHUNK
        k_row = lax.broadcasted_iota(I32, (ts, nq), 0)

        def score_tile(masked, j, carry):
            r0 = pl.multiple_of(j * ts, ts)
            ki_t = ki_ref[pl.ds(r0, ts), :]
            s = jnp.zeros((ts, nq), F32)
            for h in range(IDX_HEADS):
                s = s + wit_ref[h:h + 1, :] * jnp.maximum(_bdot(ki_t, qm_ref[h]), 0.0)
            if masked:
                s = jnp.where(((r0 + k_row) // CHUNK) <= q_chunk, s, NEG)
            key_ref[pl.ds(r0, ts), :] = _sort_key(s)
            return carry

        n_below = q0 // ts
        lax.fori_loop(0, n_below, functools.partial(score_tile, False), 0)
        lax.fori_loop(n_below, n_tiles, functools.partial(score_tile, True), 0)

        def count(ind_fn):
            def body(j, cnt):
                r0 = pl.multiple_of(j * ts, ts)
                for u in range(ts // rs):
                    ks = key_ref[pl.ds(r0 + u * rs, rs), :]
                    cnt = cnt + ind_fn(ks, r0 + u * rs)
                return cnt
            cnt = lax.fori_loop(0, n_tiles, body, jnp.zeros((rs, nq), I32))
            tot = jnp.sum(cnt.astype(F32), axis=0, keepdims=True).astype(I32)
            return jnp.broadcast_to(tot, (rs, nq))

        def bit_cond(st):
            b, _, cnt_t = st
            return (b < 32) & (jnp.max(cnt_t.astype(F32)) > float(topk))

        def bit_step(st):
            b, t, cnt_t = st
            bit = jnp.left_shift(jnp.int32(1), 31 - b)
            cand = jnp.where(b == 0, jnp.zeros_like(t), t | bit)
            cnt = (count(lambda ks, c0: jnp.where(ks >= cand, 1, 0))
                   + jnp.where(cand <= key_neg, n_out, 0))
            take = cnt >= topk
            return b + 1, jnp.where(take, cand, t), jnp.where(take, cnt, cnt_t)

        _, t, cnt_t = lax.while_loop(
            bit_cond, bit_step,
            (jnp.int32(0), jnp.full((rs, nq), INT_MIN, I32), jnp.full((rs, nq), seq_len, I32)))

        def tie_cut():
            n_gt = (count(lambda ks, c0: jnp.where(ks > t, 1, 0))
                    + jnp.where(t < key_neg, n_out, 0))
            n_eq = count(lambda ks, c0: jnp.where(ks == t, 1, 0))
            need = topk - n_gt
            real = jnp.where(t > key_neg, 1, 0)
            n_amb = jnp.sum((real * jnp.where(n_eq > need, 1, 0)).astype(F32))

            def search():
                def idx_step(b, c):
                    cand = c | jnp.left_shift(jnp.int32(1), 14 - b)
                    f = count(lambda ks, r0: jnp.where(
                        ks == t, jnp.where((r0 + sub) < cand, 1, 0), 0))
                    return jnp.where(f <= need, cand, c)
                return real * lax.fori_loop(0, 15, idx_step, jnp.zeros((rs, nq), I32))

            return lax.cond(n_amb > 0.0, search, lambda: real * seq_len)

        exact_sets = jnp.max(cnt_t.astype(F32)) <= float(topk)
        cut = lax.cond(exact_sets, lambda: jnp.full((rs, nq), seq_len, I32), tie_cut)

        def bias_tile(j, carry):
            r0 = pl.multiple_of(j * ts, ts)
            for u in range(ts // rs):
                ru = r0 + u * rs
                ks = key_ref[pl.ds(ru, rs), :]
                tie_ok = jnp.where((ru + sub) < cut, 0.0, NEG)
                bias = jnp.where(ks > t, 0.0, jnp.where(ks == t, tie_ok, NEG))
                key_ref[pl.ds(ru, rs), :] = pltpu.bitcast(bias, I32)
            return carry

        lax.fori_loop(0, n_tiles, bias_tile, 0)

    qt = qt_ref[...]
    row_head = lax.broadcasted_iota(I32, (LANES, nq), 0) // HEAD_DIM
    qh = [jnp.where(row_head == hh, qt, jnp.zeros_like(qt)) for hh in range(2)]
    ones_rows = jnp.ones((ATT_SUM_ROWS, ts), BF16)

    def logits_stage(j, s_ref):
        r0 = pl.multiple_of(j * ts, ts)
        k_t = k_ref[pl.ds(r0, ts), :]
        bias = pltpu.bitcast(key_ref[pl.ds(r0, ts), :], F32)
        tile_max = []
        for hh in range(2):
            s = _bdot(k_t, qh[hh]) + bias
            s_ref[hh] = s
            tile_max.append(jnp.max(s, axis=0, keepdims=True))
        return tuple(tile_max)

    def value_stage(j, s_ref, tile_max, carry):
        r0 = pl.multiple_of(j * ts, ts)
        new = []
        for hh in range(2):
            m, acc = carry[hh]
            m_new = jnp.maximum(m, tile_max[hh])
            p = jnp.exp2(s_ref[hh] - m_new).astype(BF16)
            vt = vt_ref[hh * HEAD_DIM:(hh + 1) * HEAD_DIM, pl.ds(r0, ts)]
            acc = jnp.exp2(m - m_new) * acc + _bdot(jnp.concatenate([vt, ones_rows], axis=0), p)
            new.append((m_new, acc))
        return tuple(new)

    def tile_pair(i, state):
        max_a, carry = state
        max_b = logits_stage(2 * i + 1, sb_ref)
        carry = value_stage(2 * i, sa_ref, max_a, carry)
        max_a = logits_stage(jnp.minimum(2 * i + 2, n_tiles - 1), sa_ref)
        carry = value_stage(2 * i + 1, sb_ref, max_b, carry)
        return max_a, carry

    head_init = (jnp.full((1, nq), NEG, F32), jnp.zeros((HEAD_DIM + ATT_SUM_ROWS, nq), F32))
    _, carry = lax.fori_loop(0, n_tiles // 2, tile_pair,
                             (logits_stage(0, sa_ref), (head_init, head_init)))
    for hh in range(2):
        _, acc = carry[hh]
        o_ref[hh * HEAD_DIM:(hh + 1) * HEAD_DIM, :] = (
            acc[:HEAD_DIM] / acc[HEAD_DIM:HEAD_DIM + 1]).astype(BF16)


def _dsa_attention(qt, k, vt, qit, ki, wit, batch, seq_len):
    d, n = qt.shape
    nq = min(ATT_QB, seq_len)
    qpb = seq_len // nq
    n_idx = IDX_HEADS * IDX_DIM
    topk = min(TOPK_MAX, seq_len // 4)
    assert seq_len % (2 * ATT_TS) == 0 and seq_len % nq == 0
    kern = functools.partial(_attn_kernel, seq_len=seq_len, topk=topk)
    return pl.pallas_call(
        kern,
        grid=(batch, qpb, d // LANES),
        in_specs=[
            pl.BlockSpec((LANES, nq), lambda b, i, h: (h, b * qpb + i)),
            pl.BlockSpec((seq_len, LANES), lambda b, i, h: (b, h)),
            pl.BlockSpec((LANES, seq_len), lambda b, i, h: (h, b)),
            pl.BlockSpec((n_idx, nq), lambda b, i, h: (0, b * qpb + i)),
            pl.BlockSpec((seq_len, LANES), lambda b, i, h: (b, 0)),
            pl.BlockSpec((IDX_HEADS, nq), lambda b, i, h: (0, b * qpb + i)),
        ],
        out_specs=pl.BlockSpec((LANES, nq), lambda b, i, h: (h, b * qpb + i)),
        out_shape=jax.ShapeDtypeStruct((d, n), BF16),
        scratch_shapes=[pltpu.VMEM((seq_len, nq), I32), pltpu.VMEM((IDX_HEADS, LANES, nq), BF16),
                        pltpu.VMEM((2, ATT_TS, nq), F32), pltpu.VMEM((2, ATT_TS, nq), F32)],
        compiler_params=_cparams("parallel", "parallel", "arbitrary"),
        name="dsa_attention",
    )(qt, k, vt, qit, ki, wit)


OUT_TM = 512


def _out_proj_kernel(x_ref, at_ref, w_ref, gate_ref, o_ref):
    y = lax.dot_general(at_ref[...], w_ref[...], (((0,), (0,)), ((), ())),
                        preferred_element_type=F32)
    o_ref[...] = x_ref[...] + gate_ref[0] * y


def _out_proj(x, a_t, w, gate, seq_len):
    n, d = x.shape
    tm = min(OUT_TM, seq_len)
    tpb = seq_len // tm
    row = pl.BlockSpec((tm, d), lambda i: (i, 0))
    return pl.pallas_call(
        _out_proj_kernel,
        grid=(n // tm,),
        in_specs=[row, pl.BlockSpec((d, tm), lambda i: (0, i)), _const_spec((d, d)), _vec_spec(tpb)],
        out_specs=row,
        out_shape=jax.ShapeDtypeStruct((n, d), F32),
        compiler_params=_cparams("parallel"),
        name="attn_out_proj",
    )(x, a_t, w, gate)


NORMT_TM = 512


def _norm_t_kernel(x_ref, g_ref, sc_ref, sh_ref, o_ref):
    o_ref[...] = _norm_mod(x_ref[...], g_ref[...], sc_ref[0], sh_ref[0]).T


def _norm_transposed(x, norm_g, sc, sh, seq_len):
    n, d = x.shape
    tm = min(NORMT_TM, seq_len)
    tpb = seq_len // tm
    return pl.pallas_call(
        _norm_t_kernel,
        grid=(n // tm,),
        in_specs=[pl.BlockSpec((tm, d), lambda i: (i, 0)), _const_spec((1, d)),
                  _vec_spec(tpb), _vec_spec(tpb)],
        out_specs=pl.BlockSpec((d, tm), lambda i: (0, i)),
        out_shape=jax.ShapeDtypeStruct((d, n), F32),
        compiler_params=_cparams("parallel"),
        name="ssm_norm_t",
    )(x, norm_g, sc, sh)


def _ssm_prep_kernel(are_ref, aim_ref, ldt_ref, brt_ref, bit_ref, cr_ref, ci_ref, crt_ref, cit_ref,
                     toep_ref, bst_ref, cout_ref, lam_ref, kt_ref):
    t_len, p, sg = SSM_T, SSM_STATE, SSM_GROUP
    a_re, a_im = are_ref[0], aim_ref[0]
    dt = jnp.exp(ldt_ref[0])
    ar, th = a_re * dt, a_im * dt
    decay = jnp.exp(ar)
    ab_re, ab_im = decay * jnp.cos(th), decay * jnp.sin(th)
    den = a_re * a_re + a_im * a_im
    nr, ni = ab_re - 1.0, ab_im
    coef_re = (nr * a_re + ni * a_im) / den
    coef_im = (ni * a_re - nr * a_im) / den
    brt, bit = brt_ref[0], bit_ref[0]
    bbr = coef_re * brt - coef_im * bit
    bbi = coef_re * bit + coef_im * brt
    cr, ci = cr_ref[0], ci_ref[0]
    crt, cit = crt_ref[0], cit_ref[0]

    def powers(nn):
        mag = jnp.exp(nn * ar)
        return mag * jnp.cos(nn * th), mag * jnp.sin(nn * th)

    n_iota = lax.broadcasted_iota(I32, (t_len, p), 0).astype(F32)
    l_re, l_im = powers(n_iota)
    l1_re, l1_im = powers(n_iota + 1.0)
    lr_re, lr_im = powers((t_len - 1.0) - n_iota)
    lcat_t = jnp.concatenate([l_re, l_im], axis=1).T
    l1cat_t = jnp.concatenate([l1_re, l1_im], axis=1).T
    l1_re_t, l1_im_t = l1cat_t[:p], l1cat_t[p:]

    mt = []
    for c in range(sg):
        m_re = cr[c:c + 1] * bbr - ci[c:c + 1] * bbi
        m_im = -(cr[c:c + 1] * bbi + ci[c:c + 1] * bbr)
        mt.append(jnp.concatenate([m_re, m_im], axis=1))
    mt = jnp.concatenate(mt, axis=0)
    kt_ref[...] = jnp.dot(mt, lcat_t, preferred_element_type=F32, precision=lax.Precision.HIGHEST)

    tau = lax.broadcasted_iota(I32, (t_len, t_len), 0)
    tt = lax.broadcasted_iota(I32, (t_len, t_len), 1)
    causal = tt >= tau

    def toep_column(c, carry):
        rows = kt_ref[pl.ds(pl.multiple_of(c * sg, sg), sg), :]
        for cp in range(sg):
            krow = jnp.broadcast_to(rows[cp:cp + 1, :], (t_len, t_len))
            tile = pltpu.roll(krow, 0, 1, stride=1, stride_axis=0)
            toep_ref[0, cp * t_len:(cp + 1) * t_len, pl.ds(pl.multiple_of(c * t_len, t_len), t_len)] = (
                jnp.where(causal, tile, 0.0).astype(BF16))
        return carry

    lax.fori_loop(0, sg, toep_column, 0)

    for c in range(sg):
        bst_ref[0, c * t_len:(c + 1) * t_len, :] = jnp.concatenate(
            [lr_re * bbr[c:c + 1] - lr_im * bbi[c:c + 1],
             lr_re * bbi[c:c + 1] + lr_im * bbr[c:c + 1]], axis=1).astype(BF16)
        ccr, cci = crt[:, c:c + 1], cit[:, c:c + 1]
        cout_ref[0, :, c * t_len:(c + 1) * t_len] = jnp.concatenate(
            [ccr * l1_re_t - cci * l1_im_t, -ccr * l1_im_t - cci * l1_re_t], axis=0).astype(BF16)

    lt_re, lt_im = powers(jnp.full((SUBLANES, p), float(t_len), F32))
    lam_ref[0] = jnp.concatenate([lt_re, lt_im], axis=1)


def _ssm_prep(a_re, a_im, log_dt, b_re, b_im, c_re, c_im):
    g, p, sg, t_len = N_GROUPS, SSM_STATE, SSM_GROUP, SSM_T
    vec = lambda a: a.reshape(g, 1, -1)
    spec3 = lambda s: pl.BlockSpec((1,) + s, lambda i: (i, 0, 0))
    return pl.pallas_call(
        _ssm_prep_kernel,
        grid=(g,),
        in_specs=[spec3((1, p)), spec3((1, p)), spec3((1, 1)), spec3((sg, p)), spec3((sg, p)),
                  spec3((sg, p)), spec3((sg, p)), spec3((p, sg)), spec3((p, sg))],
        out_specs=[spec3((sg * t_len, sg * t_len)), spec3((sg * t_len, 2 * p)),
                   spec3((2 * p, sg * t_len)), spec3((SUBLANES, 2 * p))],
        out_shape=[
            jax.ShapeDtypeStruct((g, sg * t_len, sg * t_len), BF16),
            jax.ShapeDtypeStruct((g, sg * t_len, 2 * p), BF16),
            jax.ShapeDtypeStruct((g, 2 * p, sg * t_len), BF16),
            jax.ShapeDtypeStruct((g, SUBLANES, 2 * p), F32),
        ],
        scratch_shapes=[pltpu.VMEM((sg * sg, t_len), F32)],
        compiler_params=_cparams("parallel"),
        name="ssm_prep",
    )(vec(a_re), vec(a_im), vec(log_dt), b_re.transpose(0, 2, 1), b_im.transpose(0, 2, 1),
      c_re, c_im, c_re.transpose(0, 2, 1), c_im.transpose(0, 2, 1))


def _gelu_tanh(x):
    return 0.5 * x * (1.0 + jnp.tanh(math.sqrt(2.0 / math.pi) * (x + 0.044715 * (x * x * x))))


def _ssm_kernel(u_ref, toep_ref, bst_ref, cout_ref, lam_ref, d_ref, o_ref, *, chunks_per_batch):
    sg, t_len, p = SSM_GROUP, SSM_T, SSM_STATE
    nc = u_ref.shape[1]
    lhs = jnp.concatenate([u_ref[c].astype(BF16) for c in range(sg)], axis=1)
    state = _bdot(lhs, bst_ref[0])
    row = lax.broadcasted_iota(I32, (nc, 2 * p), 0) % chunks_per_batch
    lane = lax.broadcasted_iota(I32, (1, 2 * p), 1)
    lam = lam_ref[0]
    a_re = lam[0:1, :p]
    a_im = lam[0:1, p:]

    def cmul_rows(a_re, a_im, x):
        a1 = jnp.concatenate([a_re, a_re], axis=1)
        a2 = jnp.concatenate([-a_im, a_im], axis=1)
        return a1 * x + a2 * pltpu.roll(x, p, 1)

    k = 1
    while k < chunks_per_batch:
        shifted = jnp.where(row >= k, pltpu.roll(state, k, 0), 0.0)
        state = state + cmul_rows(a_re, a_im, shifted)
        a_re, a_im = a_re * a_re - a_im * a_im, 2.0 * a_re * a_im
        k *= 2
    carry = jnp.where(row >= 1, pltpu.roll(state, 1, 0), 0.0)
    del lane
    y = _bdot(lhs, toep_ref[0]) + _bdot(carry.astype(BF16), cout_ref[0])
    for c in range(sg):
        yc = y[:, c * t_len:(c + 1) * t_len] + d_ref[0, c:c + 1, :] * u_ref[c]
        o_ref[c] = _gelu_tanh(yc).astype(BF16)


def _ssm_apply(u_t, toep, bst, cout, lam, d_skip, seq_len):
    d, n = u_t.shape
    g, sg, t_len, p = N_GROUPS, SSM_GROUP, SSM_T, SSM_STATE
    nc = n // t_len
    u3 = u_t.reshape(d, nc, t_len)
    d3 = jnp.broadcast_to(d_skip.reshape(g, sg, 1), (g, sg, t_len))
    kern = functools.partial(_ssm_kernel, chunks_per_batch=seq_len // t_len)
    spec3 = lambda s: pl.BlockSpec((1,) + s, lambda i: (i, 0, 0))
    blk = pl.BlockSpec((sg, nc, t_len), lambda i: (i, 0, 0))
    out = pl.pallas_call(
        kern,
        grid=(g,),
        in_specs=[blk, spec3((sg * t_len, sg * t_len)), spec3((sg * t_len, 2 * p)),
                  spec3((2 * p, sg * t_len)), spec3((SUBLANES, 2 * p)), spec3((sg, t_len))],
        out_specs=blk,
        out_shape=jax.ShapeDtypeStruct((d, nc, t_len), BF16),
        compiler_params=_cparams("parallel"),
        name="ssm_apply",
    )(u3, toep, bst, cout, lam, d3)
    return out.reshape(d, n)


GLU_TM = 512


def _glu_kernel(x_ref, gt_ref, w_ref, gate_ref, o_ref):
    z = lax.dot_general(gt_ref[...], w_ref[...], (((0,), (0,)), ((), ())),
                        preferred_element_type=F32)
    d = x_ref.shape[1]
    o_ref[...] = x_ref[...] + gate_ref[0] * (z[:, :d] * _sigmoid(z[:, d:]))


def _glu_residual(x, g_t, w_glu, gate, seq_len):
    n, d = x.shape
    tm = min(GLU_TM, seq_len)
    tpb = seq_len // tm
    row = pl.BlockSpec((tm, d), lambda i: (i, 0))
    return pl.pallas_call(
        _glu_kernel,
        grid=(n // tm,),
        in_specs=[row, pl.BlockSpec((d, tm), lambda i: (0, i)), _const_spec((d, 2 * d)),
                  _vec_spec(tpb)],
        out_specs=row,
        out_shape=jax.ShapeDtypeStruct((n, d), F32),
        compiler_params=_cparams("parallel"),
        name="ssm_glu",
    )(x, g_t, w_glu, gate)


def kernel(x, c, positions, ada_w, ada_b, norm_mix, norm_ffn, attn_w_in, attn_q_gain, attn_k_gain,
           attn_w_out, ssm_a_re, ssm_a_im, ssm_log_dt, ssm_b_re, ssm_b_im, ssm_c_re, ssm_c_im,
           ssm_d, ssm_w_glu, ffn_w_up, ffn_conv_w, ffn_conv_b, ffn_w_down):
    batch, seq_len, d = x.shape
    depth = ada_w.shape[0]
    n = batch * seq_len
    xs = x.reshape(n, d)
    mod = _modulation(c, ada_w, ada_b)
    cos_t, sin_a, sin_b = _rope_tables(positions)
    for i in range(depth):
        sh_m, sc_m, g_m, sh_f, sc_f, g_f = [mod[i, :, k] for k in range(6)]
        j = i // 2
        if i % 2 == 0:
            w_in = jnp.pad(attn_w_in[j], ((0, 0), (0, IN_COLS_PAD - IN_COLS))).astype(BF16)
            qt, k, vt, qit, ki, wit = _attn_proj(
                xs, norm_mix[i].reshape(1, d), sc_m, sh_m, w_in, attn_q_gain[j], attn_k_gain[j],
                cos_t, sin_a, sin_b, seq_len)
            o = _dsa_attention(qt, k, vt, qit, ki, wit, batch, seq_len)
            xs = _out_proj(xs, o, attn_w_out[j].astype(BF16), g_m, seq_len)
        else:
            u_t = _norm_transposed(xs, norm_mix[i].reshape(1, d), sc_m, sh_m, seq_len)
            toep, bst, cout, lam = _ssm_prep(ssm_a_re[j], ssm_a_im[j], ssm_log_dt[j], ssm_b_re[j],
                                             ssm_b_im[j], ssm_c_re[j], ssm_c_im[j])
            g_t = _ssm_apply(u_t, toep, bst, cout, lam, ssm_d[j], seq_len)
            xs = _glu_residual(xs, g_t, ssm_w_glu[j].astype(BF16), g_m, seq_len)
        w_up = ffn_w_up[i].astype(BF16)
        xs = _conv_ffn(xs, norm_ffn[i].reshape(1, d), sc_f, sh_f, g_f, w_up[:, :D_FF], w_up[:, D_FF:],
                       ffn_conv_w[i], ffn_conv_b[i].reshape(1, -1), ffn_w_down[i].astype(BF16),
                       seq_len)
    return xs.reshape(batch, seq_len, d)
```

```python
import functools
import math

import jax
import jax.numpy as jnp
import numpy as np
from jax import lax
from jax.experimental import pallas as pl
from jax.experimental.pallas import tpu as pltpu

F32 = jnp.float32
BF16 = jnp.bfloat16
I32 = jnp.int32
I16 = jnp.int16

LANES = 128
SUBLANES = 8
VMEM_LIMIT = 56 << 20

D_MODEL = 1024
N_HEADS = 16
HEAD_DIM = 64
IDX_HEADS = 8
IDX_DIM = 64
CHUNK = 64
TOPK_MAX = 256
ROPE_THETA = 10000.0
SSM_GROUP = 16
N_GROUPS = D_MODEL // SSM_GROUP
SSM_STATE = 64
D_FF = 2816
EPS = 1e-6
NEG = -1e30
IN_COLS = 3 * D_MODEL + IDX_HEADS * IDX_DIM + IDX_DIM + IDX_HEADS
IN_COLS_PAD = 29 * LANES
SSM_T = 128


def _cparams(*sem):
    return pltpu.CompilerParams(dimension_semantics=sem, vmem_limit_bytes=VMEM_LIMIT)


def _sigmoid(x):
    return 1.0 / (1.0 + jnp.exp(-x))


def _norm_mod(x, g, sc, sh):
    ms = jnp.mean(x * x, axis=-1, keepdims=True)
    return (x * lax.rsqrt(ms + EPS) * g) * (1.0 + sc) + sh


def _bdot(a, b):
    return jnp.dot(a, b, preferred_element_type=F32)


def _mod_kernel(c_ref, w_ref, b_ref, o_ref):
    c = c_ref[...]
    cond = c * _sigmoid(c)
    o_ref[0] = _bdot(cond.astype(BF16), w_ref[0].astype(BF16)) + b_ref[0]


def _modulation(c, ada_w, ada_b):
    depth, d, six_d = ada_w.shape
    b = c.shape[0]
    c_pad = jnp.zeros((SUBLANES, d), F32).at[:b].set(c)
    out = pl.pallas_call(
        _mod_kernel,
        grid=(depth, six_d // d),
        in_specs=[
            pl.BlockSpec((SUBLANES, d), lambda i, j: (0, 0)),
            pl.BlockSpec((1, d, d), lambda i, j: (i, 0, j)),
            pl.BlockSpec((1, 1, d), lambda i, j: (i, 0, j)),
        ],
        out_specs=pl.BlockSpec((1, SUBLANES, d), lambda i, j: (i, 0, j)),
        out_shape=jax.ShapeDtypeStruct((depth, SUBLANES, six_d), F32),
        compiler_params=_cparams("parallel", "parallel"),
        name="adaln_mod",
    )(c_pad, ada_w, ada_b.reshape(depth, 1, six_d))
    return out[:, :b, :].reshape(depth, b, 6, 1, d)


def _vec_spec(rows_per_batch_tiles):
    return pl.BlockSpec((1, 1, D_MODEL), lambda i: (i // rows_per_batch_tiles, 0, 0))


def _const_spec(shape):
    nd = len(shape)
    return pl.BlockSpec(shape, lambda i: (0,) * nd)


FFN_TM = 512
FFN_FC = 256


def _ffn_kernel(x_ref, xp_ref, g_ref, sc_ref, sh_ref, gate_ref, wv_ref, wg_ref, cw_ref, cb_ref,
                wd_ref, o_ref, *, tiles_per_batch):
    i = pl.program_id(0)
    x = x_ref[...]
    g, sc, sh = g_ref[...], sc_ref[0], sh_ref[0]
    h = _norm_mod(x, g, sc, sh).astype(BF16)
    hp = _norm_mod(xp_ref[...], g, sc, sh).astype(BF16)
    keep_prev = (i % tiles_per_batch != 0).astype(F32)
    tm = x.shape[0]
    row8 = lax.broadcasted_iota(I32, (SUBLANES, FFN_FC), 0)

    def up_proj(w_ref, f):
        cs = slice(f * FFN_FC, (f + 1) * FFN_FC)
        return _bdot(h, w_ref[:, cs]), _bdot(hp, w_ref[:, cs]) * keep_prev

    def conv(ups, f, col0):
        up, upp = ups
        p7 = upp[7:8, :]
        p6 = upp[6:7, :]
        up1 = pltpu.roll(up, 1, 0)
        up2 = pltpu.roll(up, 2, 0)
        top1 = jnp.where(row8 == 0, p7, up1[:SUBLANES])
        top2 = jnp.where(row8 == 0, p6, jnp.where(row8 == 1, p7, up2[:SUBLANES]))
        up1 = jnp.concatenate([top1, up1[SUBLANES:]], axis=0)
        up2 = jnp.concatenate([top2, up2[SUBLANES:]], axis=0)
        wsl = slice(col0 + f * FFN_FC, col0 + (f + 1) * FFN_FC)
        cw = cw_ref[:, wsl]
        return up2 * cw[0:1] + up1 * cw[1:2] + up * cw[2:3] + cb_ref[:, wsl]

    acc = jnp.zeros((tm, D_MODEL), F32)
    n_f = D_FF // FFN_FC
    ups = (up_proj(wv_ref, 0), up_proj(wg_ref, 0))
    for f in range(n_f):
        cur = ups
        if f + 1 < n_f:
            ups = (up_proj(wv_ref, f + 1), up_proj(wg_ref, f + 1))
        val = conv(cur[0], f, 0)
        gt = conv(cur[1], f, D_FF)
        act = (gt * _sigmoid(gt) * val).astype(BF16)
        acc = acc + _bdot(act, wd_ref[f * FFN_FC:(f + 1) * FFN_FC, :])
    o_ref[...] = x + gate_ref[0] * acc


def _conv_ffn(x, norm_g, sc, sh, gate, w_val, w_gate, conv_w, conv_b, w_down, seq_len):
    n, d = x.shape
    tm = min(FFN_TM, seq_len)
    tpb = seq_len // tm
    kern = functools.partial(_ffn_kernel, tiles_per_batch=tpb)
    return pl.pallas_call(
        kern,
        grid=(n // tm,),
        in_specs=[
            pl.BlockSpec((tm, d), lambda i: (i, 0)),
            pl.BlockSpec((SUBLANES, d), lambda i: (jnp.maximum(i * (tm // SUBLANES) - 1, 0), 0)),
            _const_spec((1, d)),
            _vec_spec(tpb), _vec_spec(tpb), _vec_spec(tpb),
            _const_spec((d, D_FF)), _const_spec((d, D_FF)),
            _const_spec((3, 2 * D_FF)), _const_spec((1, 2 * D_FF)),
            _const_spec((D_FF, d)),
        ],
        out_specs=pl.BlockSpec((tm, d), lambda i: (i, 0)),
        out_shape=jax.ShapeDtypeStruct((n, d), F32),
        compiler_params=_cparams("parallel"),
        name="conv_ffn",
    )(x, x, norm_g, sc, sh, gate, w_val, w_gate, conv_w, conv_b, w_down)


ROPE_TM = 512


def _rope_kernel(pos_ref, inv_ref, cos_ref, sa_ref, sb_ref):
    ang = pos_ref[...].astype(F32) * inv_ref[...]
    cs = jnp.cos(ang)
    sn = jnp.sin(ang)
    lane = lax.broadcasted_iota(I32, ang.shape, 1)
    low = (lane % HEAD_DIM) < (HEAD_DIM // 2)
    cos_ref[...] = cs
    sa_ref[...] = jnp.where(low, -sn, 0.0)
    sb_ref[...] = jnp.where(low, 0.0, sn)


def _rope_tables(positions):
    n = positions.size
    tm = min(ROPE_TM, n)
    half = HEAD_DIM // 2
    inv = ROPE_THETA ** (-jnp.arange(half, dtype=F32) / half)
    inv = jnp.tile(inv, LANES // half).reshape(1, LANES)
    spec = pl.BlockSpec((tm, LANES), lambda i: (i, 0))
    return pl.pallas_call(
        _rope_kernel,
        grid=(n // tm,),
        in_specs=[pl.BlockSpec((tm, 1), lambda i: (i, 0)), _const_spec((1, LANES))],
        out_specs=[spec, spec, spec],
        out_shape=[jax.ShapeDtypeStruct((n, LANES), F32)] * 3,
        compiler_params=_cparams("parallel"),
        name="rope_tables",
    )(positions.reshape(n, 1), inv)


def _rope(x, cos_t, sin_a, sin_b):
    return (x * cos_t + pltpu.roll(x, LANES - HEAD_DIM // 2, 1) * sin_a
            + pltpu.roll(x, HEAD_DIM // 2, 1) * sin_b)


PROJ_TM = 256


def _attn_proj_kernel(x_ref, g_ref, sc_ref, sh_ref, w_ref, qg_ref, kg_ref, cos_ref, sa_ref, sb_ref,
                      hm_ref, qt_ref, k_ref, vt_ref, qit_ref, ki_ref, wit_ref):
    h = _norm_mod(x_ref[...], g_ref[...], sc_ref[0], sh_ref[0]).astype(BF16)
    cos_t, sin_a, sin_b = cos_ref[...], sa_ref[...], sb_ref[...]
    hm = hm_ref[...]
    n_qk = D_MODEL // LANES

    def group(gi):
        return _bdot(h, w_ref[:, gi * LANES:(gi + 1) * LANES])

    def head_norm(p, gain):
        sq = p * p
        hi = sq.astype(BF16)
        lo = (sq - hi.astype(F32)).astype(BF16)
        ms = _bdot(hi, hm) + _bdot(lo, hm)
        return p * lax.rsqrt(ms + EPS) * gain

    for gi in range(n_qk):
        rows = slice(gi * LANES, (gi + 1) * LANES)
        q = _rope(head_norm(group(gi), qg_ref[...]), cos_t, sin_a, sin_b)
        qt_ref[rows, :] = (q * (HEAD_DIM ** -0.5 * LOG2E)).T.astype(BF16)
        k = _rope(head_norm(group(n_qk + gi), kg_ref[...]), cos_t, sin_a, sin_b)
        k_ref[:, rows] = k.astype(BF16)
        vt_ref[rows, :] = group(2 * n_qk + gi).T.astype(BF16)
    n_qi = IDX_HEADS * IDX_DIM // LANES
    for gi in range(n_qi):
        qi = _rope(group(3 * n_qk + gi), cos_t, sin_a, sin_b)
        qit_ref[gi * LANES:(gi + 1) * LANES, :] = qi.T.astype(BF16)
    tail = group(3 * n_qk + n_qi)
    ki = _rope(tail, cos_t, sin_a, sin_b)
    lane = lax.broadcasted_iota(I32, ki.shape, 1)
    ki_ref[...] = jnp.where(lane < IDX_DIM, ki, pltpu.roll(ki, IDX_DIM, 1)).astype(BF16)
    wi_t = (tail * (IDX_HEADS ** -0.5 * IDX_DIM ** -0.5)).T
    wit_ref[...] = wi_t[IDX_DIM:IDX_DIM + IDX_HEADS, :]


def _attn_proj(x, norm_g, sc, sh, w_in, q_gain, k_gain, cos_t, sin_a, sin_b, seq_len):
    n, d = x.shape
    tm = min(PROJ_TM, seq_len)
    tpb = seq_len // tm
    lane = jnp.arange(LANES)
    hm = ((lane[:, None] // HEAD_DIM) == (lane[None, :] // HEAD_DIM)).astype(BF16) / HEAD_DIM
    qg2 = jnp.tile(q_gain, LANES // HEAD_DIM).reshape(1, LANES)
    kg2 = jnp.tile(k_gain, LANES // HEAD_DIM).reshape(1, LANES)
    row = lambda w: pl.BlockSpec((tm, w), lambda i: (i, 0))
    col = lambda h: pl.BlockSpec((h, tm), lambda i: (0, i))
    n_idx = IDX_HEADS * IDX_DIM
    return pl.pallas_call(
        _attn_proj_kernel,
        grid=(n // tm,),
        in_specs=[
            row(d), _const_spec((1, d)), _vec_spec(tpb), _vec_spec(tpb),
            _const_spec((d, IN_COLS_PAD)), _const_spec((1, LANES)), _const_spec((1, LANES)),
            row(LANES), row(LANES), row(LANES), _const_spec((LANES, LANES)),
        ],
        out_specs=[col(d), row(d), col(d), col(n_idx), row(LANES), col(IDX_HEADS)],
        out_shape=[
            jax.ShapeDtypeStruct((d, n), BF16),
            jax.ShapeDtypeStruct((n, d), BF16),
            jax.ShapeDtypeStruct((d, n), BF16),
            jax.ShapeDtypeStruct((n_idx, n), BF16),
            jax.ShapeDtypeStruct((n, LANES), BF16),
            jax.ShapeDtypeStruct((IDX_HEADS, n), F32),
        ],
        compiler_params=_cparams("parallel"),
        name="attn_proj",
    )(x, norm_g, sc, sh, w_in, qg2, kg2, cos_t, sin_a, sin_b, hm)


ATT_QB = 256
ATT_TS = 512
ATT_RS = 64
ATT_SUM_ROWS = 16
LOG2E = math.log2(math.e)
I16_MIN = -(2 ** 15)


def _sort_key(s):
    bits = pltpu.bitcast(s + 0.0, I32)
    return bits ^ ((bits >> 31) & 0x7FFFFFFF)


def _key_halves(key):
    return (key >> 16).astype(I16), (key ^ 0x8000).astype(I16)


def _neg_key_halves():
    bits = int(np.float32(NEG).view(np.int32))
    key = bits ^ ((bits >> 31) & 0x7FFFFFFF)
    return key >> 16, (key & 0xFFFF) - 2 ** 15


NEG_KEY_HI, NEG_KEY_LO = _neg_key_halves()


def _attn_kernel(qt_ref, k_ref, vt_ref, qit_ref, ki_ref, wit_ref, o_ref, hi_ref, lo_ref, qm_ref,
                 sa_ref, sb_ref, *, seq_len, topk):
    qb = pl.program_id(1)
    hp = pl.program_id(2)
    nq, ts, rs = qt_ref.shape[1], ATT_TS, ATT_RS
    q0 = qb * nq
    n_tiles = 2 * ((q0 + nq + 2 * ts - 1) // (2 * ts))
    n_out = seq_len - n_tiles * ts
    sub = lax.broadcasted_iota(I32, (rs, nq), 0)

    @pl.when(hp == 0)
    def _select():
        row_head = lax.broadcasted_iota(I32, (LANES, nq), 0) // IDX_DIM
        for h in range(IDX_HEADS):
            pair = qit_ref[(h // 2) * LANES:(h // 2 + 1) * LANES, :]
            qm_ref[h] = jnp.where(row_head == (h % 2), pair, jnp.zeros_like(pair))
        q_chunk = (q0 + lax.broadcasted_iota(I32, (ts, nq), 1)) // CHUNK
        k_row = lax.broadcasted_iota(I32, (ts, nq), 0)

        def score_tile(masked, j, carry):
            r0 = pl.multiple_of(j * ts, ts)
            ki_t = ki_ref[pl.ds(r0, ts), :]
            s = jnp.zeros((ts, nq), F32)
            for h in range(IDX_HEADS):
                s = s + wit_ref[h:h + 1, :] * jnp.maximum(_bdot(ki_t, qm_ref[h]), 0.0)
            if masked:
                s = jnp.where(((r0 + k_row) // CHUNK) <= q_chunk, s, NEG)
            hi_ref[pl.ds(r0, ts), :], lo_ref[pl.ds(r0, ts), :] = _key_halves(_sort_key(s))
            return carry

        n_below = q0 // ts
        lax.fori_loop(0, n_below, functools.partial(score_tile, False), 0)
        lax.fori_loop(n_below, n_tiles, functools.partial(score_tile, True), 0)

        one16, zero16 = jnp.ones((rs, nq), I16), jnp.zeros((rs, nq), I16)

        def count16(ind_fn):
            def body(j, cnt):
                r0 = pl.multiple_of(j * ts, ts)
                for u in range(ts // rs):
                    cnt = cnt + ind_fn(r0 + u * rs)
                return cnt
            cnt = lax.fori_loop(0, n_tiles, body, zero16)
            tot = jnp.sum(cnt.astype(F32), axis=0, keepdims=True).astype(I32)
            return jnp.broadcast_to(tot, (rs, nq))

        def search16(ref, target, outside_fn, count_init):
            def cond(st):
                b, _, cnt_t = st
                return (b < 16) & (jnp.max((cnt_t - target).astype(F32)) > 0.0)

            def step(st):
                b, t, cnt_t = st
                bit = jnp.left_shift(jnp.int32(1), 15 - b)
                cand = jnp.where(b == 0, jnp.zeros_like(t), t | bit)
                c16 = cand.astype(I16)
                cnt = count16(lambda ru: jnp.where(ref[pl.ds(ru, rs), :] >= c16, one16, zero16))
                cnt = cnt + outside_fn(cand)
                take = cnt >= target
                return b + 1, jnp.where(take, cand, t), jnp.where(take, cnt, cnt_t)

            _, t, cnt_t = lax.while_loop(
                cond, step, (jnp.int32(0), jnp.full((rs, nq), I16_MIN, I32), count_init))
            return t, cnt_t

        k_arr = jnp.full((rs, nq), topk, I32)
        t_hi, cnt_hi = search16(hi_ref, k_arr, lambda c: jnp.where(c <= NEG_KEY_HI, n_out, 0),
                                jnp.full((rs, nq), seq_len, I32))
        t_hi16 = t_hi.astype(I16)
        n_above = (count16(lambda ru: jnp.where(hi_ref[pl.ds(ru, rs), :] > t_hi16, one16, zero16))
                   + jnp.where(t_hi < NEG_KEY_HI, n_out, 0))
        need_lo = topk - n_above
        out_match = jnp.where(t_hi == NEG_KEY_HI, n_out, 0)

        def mark_tile(j, carry):
            r0 = pl.multiple_of(j * ts, ts)
            for u in range(ts // rs):
                rows = pl.ds(r0 + u * rs, rs)
                lo_ref[rows, :] = jnp.where(hi_ref[rows, :] == t_hi16, lo_ref[rows, :],
                                            jnp.full((rs, nq), I16_MIN, I16))
            return carry

        lax.fori_loop(0, n_tiles, mark_tile, 0)
        t_lo, cnt_lo = search16(lo_ref, need_lo, lambda c: jnp.where(c <= NEG_KEY_LO, out_match, 0),
                                cnt_hi - n_above)
        t_lo16 = t_lo.astype(I16)
        sub16 = sub.astype(I16)

        def tie_cut():
            n_gt = (count16(lambda ru: jnp.where(lo_ref[pl.ds(ru, rs), :] > t_lo16, one16, zero16))
                    + jnp.where(t_lo < NEG_KEY_LO, out_match, 0))
            need = need_lo - n_gt
            n_eq = cnt_lo - n_gt
            real = jnp.where(t_hi > NEG_KEY_HI, 1,
                             jnp.where(t_hi == NEG_KEY_HI, jnp.where(t_lo > NEG_KEY_LO, 1, 0), 0))
            n_amb = jnp.sum((real * jnp.where(n_eq > need, 1, 0)).astype(F32))

            def search():
                def idx_step(b, c):
                    cand = c | jnp.left_shift(jnp.int32(1), 14 - b)
                    c16 = cand.astype(I16)

                    def ind(ru):
                        rows = pl.ds(ru, rs)
                        below = jnp.where((sub16 + ru.astype(I16)) < c16, one16, zero16)
                        tied = jnp.where(lo_ref[rows, :] == t_lo16, below, zero16)
                        return jnp.where(hi_ref[rows, :] == t_hi16, tied, zero16)

                    return jnp.where(count16(ind) <= need, cand, c)
                return real * lax.fori_loop(0, 15, idx_step, jnp.zeros((rs, nq), I32))

            return lax.cond(n_amb > 0.0, search, lambda: real * seq_len)

        exact_sets = jnp.max((cnt_lo - need_lo).astype(F32)) <= 0.0
        cut16 = lax.cond(exact_sets, lambda: jnp.full((rs, nq), seq_len, I32), tie_cut).astype(I16)
        zero_b, neg_b = jnp.zeros((rs, nq), BF16), jnp.full((rs, nq), NEG, BF16)

        def bias_tile(j, carry):
            r0 = pl.multiple_of(j * ts, ts)
            for u in range(ts // rs):
                ru = r0 + u * rs
                rows = pl.ds(ru, rs)
                hi, lo = hi_ref[rows, :], lo_ref[rows, :]
                tie_ok = jnp.where((sub16 + ru.astype(I16)) < cut16, zero_b, neg_b)
                at_hi = jnp.where(lo > t_lo16, zero_b, jnp.where(lo == t_lo16, tie_ok, neg_b))
                bias = jnp.where(hi > t_hi16, zero_b, jnp.where(hi == t_hi16, at_hi, neg_b))
                hi_ref[rows, :] = pltpu.bitcast(bias, I16)
            return carry

        lax.fori_loop(0, n_tiles, bias_tile, 0)

    qt = qt_ref[...]
    row_head = lax.broadcasted_iota(I32, (LANES, nq), 0) // HEAD_DIM
    qh = [jnp.where(row_head == hh, qt, jnp.zeros_like(qt)) for hh in range(2)]
    ones_rows = jnp.ones((ATT_SUM_ROWS, ts), BF16)

    def logits_stage(j, s_ref):
        r0 = pl.multiple_of(j * ts, ts)
        k_t = k_ref[pl.ds(r0, ts), :]
        bias = pltpu.bitcast(hi_ref[pl.ds(r0, ts), :], BF16).astype(F32)
        tile_max = []
        for hh in range(2):
            s = _bdot(k_t, qh[hh]) + bias
            s_ref[hh] = s
            tile_max.append(jnp.max(s, axis=0, keepdims=True))
        return tuple(tile_max)

    def value_stage(j, s_ref, tile_max, carry):
        r0 = pl.multiple_of(j * ts, ts)
        new = []
        for hh in range(2):
            m, acc = carry[hh]
            m_new = jnp.maximum(m, tile_max[hh])
            p = jnp.exp2(s_ref[hh] - m_new).astype(BF16)
            vt = vt_ref[hh * HEAD_DIM:(hh + 1) * HEAD_DIM, pl.ds(r0, ts)]
            acc = jnp.exp2(m - m_new) * acc + _bdot(jnp.concatenate([vt, ones_rows], axis=0), p)
            new.append((m_new, acc))
        return tuple(new)

    def tile_pair(i, state):
        max_a, carry = state
        max_b = logits_stage(2 * i + 1, sb_ref)
        carry = value_stage(2 * i, sa_ref, max_a, carry)
        max_a = logits_stage(jnp.minimum(2 * i + 2, n_tiles - 1), sa_ref)
        carry = value_stage(2 * i + 1, sb_ref, max_b, carry)
        return max_a, carry

    head_init = (jnp.full((1, nq), NEG, F32), jnp.zeros((HEAD_DIM + ATT_SUM_ROWS, nq), F32))
    _, carry = lax.fori_loop(0, n_tiles // 2, tile_pair,
                             (logits_stage(0, sa_ref), (head_init, head_init)))
    for hh in range(2):
        _, acc = carry[hh]
        o_ref[hh * HEAD_DIM:(hh + 1) * HEAD_DIM, :] = (
            acc[:HEAD_DIM] / acc[HEAD_DIM:HEAD_DIM + 1]).astype(BF16)


def _dsa_attention(qt, k, vt, qit, ki, wit, batch, seq_len):
    d, n = qt.shape
    nq = min(ATT_QB, seq_len)
    qpb = seq_len // nq
    n_idx = IDX_HEADS * IDX_DIM
    topk = min(TOPK_MAX, seq_len // 4)
    assert seq_len % (2 * ATT_TS) == 0 and seq_len % nq == 0
    kern = functools.partial(_attn_kernel, seq_len=seq_len, topk=topk)
    return pl.pallas_call(
        kern,
        grid=(batch, qpb, d // LANES),
        in_specs=[
            pl.BlockSpec((LANES, nq), lambda b, i, h: (h, b * qpb + i)),
            pl.BlockSpec((seq_len, LANES), lambda b, i, h: (b, h)),
            pl.BlockSpec((LANES, seq_len), lambda b, i, h: (h, b)),
            pl.BlockSpec((n_idx, nq), lambda b, i, h: (0, b * qpb + i)),
            pl.BlockSpec((seq_len, LANES), lambda b, i, h: (b, 0)),
            pl.BlockSpec((IDX_HEADS, nq), lambda b, i, h: (0, b * qpb + i)),
        ],
        out_specs=pl.BlockSpec((LANES, nq), lambda b, i, h: (h, b * qpb + i)),
        out_shape=jax.ShapeDtypeStruct((d, n), BF16),
        scratch_shapes=[pltpu.VMEM((seq_len, nq), I16), pltpu.VMEM((seq_len, nq), I16),
                        pltpu.VMEM((IDX_HEADS, LANES, nq), BF16),
                        pltpu.VMEM((2, ATT_TS, nq), F32), pltpu.VMEM((2, ATT_TS, nq), F32)],
        compiler_params=_cparams("parallel", "parallel", "arbitrary"),
        name="dsa_attention",
    )(qt, k, vt, qit, ki, wit)


OUT_TM = 512


def _out_proj_kernel(x_ref, at_ref, w_ref, gate_ref, o_ref):
    y = lax.dot_general(at_ref[...], w_ref[...], (((0,), (0,)), ((), ())),
                        preferred_element_type=F32)
    o_ref[...] = x_ref[...] + gate_ref[0] * y


def _out_proj(x, a_t, w, gate, seq_len):
    n, d = x.shape
    tm = min(OUT_TM, seq_len)
    tpb = seq_len // tm
    row = pl.BlockSpec((tm, d), lambda i: (i, 0))
    return pl.pallas_call(
        _out_proj_kernel,
        grid=(n // tm,),
        in_specs=[row, pl.BlockSpec((d, tm), lambda i: (0, i)), _const_spec((d, d)), _vec_spec(tpb)],
        out_specs=row,
        out_shape=jax.ShapeDtypeStruct((n, d), F32),
        compiler_params=_cparams("parallel"),
        name="attn_out_proj",
    )(x, a_t, w, gate)


NORMT_TM = 512


def _norm_t_kernel(x_ref, g_ref, sc_ref, sh_ref, o_ref):
    o_ref[...] = _norm_mod(x_ref[...], g_ref[...], sc_ref[0], sh_ref[0]).T


def _norm_transposed(x, norm_g, sc, sh, seq_len):
    n, d = x.shape
    tm = min(NORMT_TM, seq_len)
    tpb = seq_len // tm
    return pl.pallas_call(
        _norm_t_kernel,
        grid=(n // tm,),
        in_specs=[pl.BlockSpec((tm, d), lambda i: (i, 0)), _const_spec((1, d)),
                  _vec_spec(tpb), _vec_spec(tpb)],
        out_specs=pl.BlockSpec((d, tm), lambda i: (0, i)),
        out_shape=jax.ShapeDtypeStruct((d, n), F32),
        compiler_params=_cparams("parallel"),
        name="ssm_norm_t",
    )(x, norm_g, sc, sh)


def _ssm_prep_kernel(are_ref, aim_ref, ldt_ref, brt_ref, bit_ref, cr_ref, ci_ref, crt_ref, cit_ref,
                     toep_ref, bst_ref, cout_ref, lam_ref, kt_ref):
    t_len, p, sg = SSM_T, SSM_STATE, SSM_GROUP
    a_re, a_im = are_ref[0], aim_ref[0]
    dt = jnp.exp(ldt_ref[0])
    ar, th = a_re * dt, a_im * dt
    decay = jnp.exp(ar)
    ab_re, ab_im = decay * jnp.cos(th), decay * jnp.sin(th)
    den = a_re * a_re + a_im * a_im
    nr, ni = ab_re - 1.0, ab_im
    coef_re = (nr * a_re + ni * a_im) / den
    coef_im = (ni * a_re - nr * a_im) / den
    brt, bit = brt_ref[0], bit_ref[0]
    bbr = coef_re * brt - coef_im * bit
    bbi = coef_re * bit + coef_im * brt
    cr, ci = cr_ref[0], ci_ref[0]
    crt, cit = crt_ref[0], cit_ref[0]

    def powers(nn):
        mag = jnp.exp(nn * ar)
        return mag * jnp.cos(nn * th), mag * jnp.sin(nn * th)

    n_iota = lax.broadcasted_iota(I32, (t_len, p), 0).astype(F32)
    l_re, l_im = powers(n_iota)
    l1_re, l1_im = powers(n_iota + 1.0)
    lr_re, lr_im = powers((t_len - 1.0) - n_iota)
    lcat_t = jnp.concatenate([l_re, l_im], axis=1).T
    l1cat_t = jnp.concatenate([l1_re, l1_im], axis=1).T
    l1_re_t, l1_im_t = l1cat_t[:p], l1cat_t[p:]

    mt = []
    for c in range(sg):
        m_re = cr[c:c + 1] * bbr - ci[c:c + 1] * bbi
        m_im = -(cr[c:c + 1] * bbi + ci[c:c + 1] * bbr)
        mt.append(jnp.concatenate([m_re, m_im], axis=1))
    mt = jnp.concatenate(mt, axis=0)
    kt_ref[...] = jnp.dot(mt, lcat_t, preferred_element_type=F32, precision=lax.Precision.HIGHEST)

    tau = lax.broadcasted_iota(I32, (t_len, t_len), 0)
    tt = lax.broadcasted_iota(I32, (t_len, t_len), 1)
    causal = tt >= tau

    def toep_column(c, carry):
        rows = kt_ref[pl.ds(pl.multiple_of(c * sg, sg), sg), :]
        for cp in range(sg):
            krow = jnp.broadcast_to(rows[cp:cp + 1, :], (t_len, t_len))
            tile = pltpu.roll(krow, 0, 1, stride=1, stride_axis=0)
            toep_ref[0, cp * t_len:(cp + 1) * t_len, pl.ds(pl.multiple_of(c * t_len, t_len), t_len)] = (
                jnp.where(causal, tile, 0.0).astype(BF16))
        return carry

    lax.fori_loop(0, sg, toep_column, 0)

    for c in range(sg):
        bst_ref[0, c * t_len:(c + 1) * t_len, :] = jnp.concatenate(
            [lr_re * bbr[c:c + 1] - lr_im * bbi[c:c + 1],
             lr_re * bbi[c:c + 1] + lr_im * bbr[c:c + 1]], axis=1).astype(BF16)
        ccr, cci = crt[:, c:c + 1], cit[:, c:c + 1]
        cout_ref[0, :, c * t_len:(c + 1) * t_len] = jnp.concatenate(
            [ccr * l1_re_t - cci * l1_im_t, -ccr * l1_im_t - cci * l1_re_t], axis=0).astype(BF16)

    lt_re, lt_im = powers(jnp.full((SUBLANES, p), float(t_len), F32))
    lam_ref[0] = jnp.concatenate([lt_re, lt_im], axis=1)


def _ssm_prep(a_re, a_im, log_dt, b_re, b_im, c_re, c_im):
    g, p, sg, t_len = N_GROUPS, SSM_STATE, SSM_GROUP, SSM_T
    vec = lambda a: a.reshape(g, 1, -1)
    spec3 = lambda s: pl.BlockSpec((1,) + s, lambda i: (i, 0, 0))
    return pl.pallas_call(
        _ssm_prep_kernel,
        grid=(g,),
        in_specs=[spec3((1, p)), spec3((1, p)), spec3((1, 1)), spec3((sg, p)), spec3((sg, p)),
                  spec3((sg, p)), spec3((sg, p)), spec3((p, sg)), spec3((p, sg))],
        out_specs=[spec3((sg * t_len, sg * t_len)), spec3((sg * t_len, 2 * p)),
                   spec3((2 * p, sg * t_len)), spec3((SUBLANES, 2 * p))],
        out_shape=[
            jax.ShapeDtypeStruct((g, sg * t_len, sg * t_len), BF16),
            jax.ShapeDtypeStruct((g, sg * t_len, 2 * p), BF16),
            jax.ShapeDtypeStruct((g, 2 * p, sg * t_len), BF16),
            jax.ShapeDtypeStruct((g, SUBLANES, 2 * p), F32),
        ],
        scratch_shapes=[pltpu.VMEM((sg * sg, t_len), F32)],
        compiler_params=_cparams("parallel"),
        name="ssm_prep",
    )(vec(a_re), vec(a_im), vec(log_dt), b_re.transpose(0, 2, 1), b_im.transpose(0, 2, 1),
      c_re, c_im, c_re.transpose(0, 2, 1), c_im.transpose(0, 2, 1))


def _gelu_tanh(x):
    return 0.5 * x * (1.0 + jnp.tanh(math.sqrt(2.0 / math.pi) * (x + 0.044715 * (x * x * x))))


def _ssm_kernel(u_ref, toep_ref, bst_ref, cout_ref, lam_ref, d_ref, o_ref, *, chunks_per_batch):
    sg, t_len, p = SSM_GROUP, SSM_T, SSM_STATE
    nc = u_ref.shape[1]
    lhs = jnp.concatenate([u_ref[c].astype(BF16) for c in range(sg)], axis=1)
    state = _bdot(lhs, bst_ref[0])
    row = lax.broadcasted_iota(I32, (nc, 2 * p), 0) % chunks_per_batch
    lane = lax.broadcasted_iota(I32, (1, 2 * p), 1)
    lam = lam_ref[0]
    a_re = lam[0:1, :p]
    a_im = lam[0:1, p:]

    def cmul_rows(a_re, a_im, x):
        a1 = jnp.concatenate([a_re, a_re], axis=1)
        a2 = jnp.concatenate([-a_im, a_im], axis=1)
        return a1 * x + a2 * pltpu.roll(x, p, 1)

    k = 1
    while k < chunks_per_batch:
        shifted = jnp.where(row >= k, pltpu.roll(state, k, 0), 0.0)
        state = state + cmul_rows(a_re, a_im, shifted)
        a_re, a_im = a_re * a_re - a_im * a_im, 2.0 * a_re * a_im
        k *= 2
    carry = jnp.where(row >= 1, pltpu.roll(state, 1, 0), 0.0)
    del lane
    y = _bdot(lhs, toep_ref[0]) + _bdot(carry.astype(BF16), cout_ref[0])
    for c in range(sg):
        yc = y[:, c * t_len:(c + 1) * t_len] + d_ref[0, c:c + 1, :] * u_ref[c]
        o_ref[c] = _gelu_tanh(yc).astype(BF16)


def _ssm_apply(u_t, toep, bst, cout, lam, d_skip, seq_len):
    d, n = u_t.shape
    g, sg, t_len, p = N_GROUPS, SSM_GROUP, SSM_T, SSM_STATE
    nc = n // t_len
    u3 = u_t.reshape(d, nc, t_len)
    d3 = jnp.broadcast_to(d_skip.reshape(g, sg, 1), (g, sg, t_len))
    kern = functools.partial(_ssm_kernel, chunks_per_batch=seq_len // t_len)
    spec3 = lambda s: pl.BlockSpec((1,) + s, lambda i: (i, 0, 0))
    blk = pl.BlockSpec((sg, nc, t_len), lambda i: (i, 0, 0))
    out = pl.pallas_call(
        kern,
        grid=(g,),
        in_specs=[blk, spec3((sg * t_len, sg * t_len)), spec3((sg * t_len, 2 * p)),
                  spec3((2 * p, sg * t_len)), spec3((SUBLANES, 2 * p)), spec3((sg, t_len))],
        out_specs=blk,
        out_shape=jax.ShapeDtypeStruct((d, nc, t_len), BF16),
        compiler_params=_cparams("parallel"),
        name="ssm_apply",
    )(u3, toep, bst, cout, lam, d3)
    return out.reshape(d, n)


GLU_TM = 512


def _glu_kernel(x_ref, gt_ref, w_ref, gate_ref, o_ref):
    z = lax.dot_general(gt_ref[...], w_ref[...], (((0,), (0,)), ((), ())),
                        preferred_element_type=F32)
    d = x_ref.shape[1]
    o_ref[...] = x_ref[...] + gate_ref[0] * (z[:, :d] * _sigmoid(z[:, d:]))


def _glu_residual(x, g_t, w_glu, gate, seq_len):
    n, d = x.shape
    tm = min(GLU_TM, seq_len)
    tpb = seq_len // tm
    row = pl.BlockSpec((tm, d), lambda i: (i, 0))
    return pl.pallas_call(
        _glu_kernel,
        grid=(n // tm,),
        in_specs=[row, pl.BlockSpec((d, tm), lambda i: (0, i)), _const_spec((d, 2 * d)),
                  _vec_spec(tpb)],
        out_specs=row,
        out_shape=jax.ShapeDtypeStruct((n, d), F32),
        compiler_params=_cparams("parallel"),
        name="ssm_glu",
    )(x, g_t, w_glu, gate)


def kernel(x, c, positions, ada_w, ada_b, norm_mix, norm_ffn, attn_w_in, attn_q_gain, attn_k_gain,
           attn_w_out, ssm_a_re, ssm_a_im, ssm_log_dt, ssm_b_re, ssm_b_im, ssm_c_re, ssm_c_im,
           ssm_d, ssm_w_glu, ffn_w_up, ffn_conv_w, ffn_conv_b, ffn_w_down):
    batch, seq_len, d = x.shape
    depth = ada_w.shape[0]
    n = batch * seq_len
    xs = x.reshape(n, d)
    mod = _modulation(c, ada_w, ada_b)
    cos_t, sin_a, sin_b = _rope_tables(positions)
    for i in range(depth):
        sh_m, sc_m, g_m, sh_f, sc_f, g_f = [mod[i, :, k] for k in range(6)]
        j = i // 2
        if i % 2 == 0:
            w_in = jnp.pad(attn_w_in[j], ((0, 0), (0, IN_COLS_PAD - IN_COLS))).astype(BF16)
            qt, k, vt, qit, ki, wit = _attn_proj(
                xs, norm_mix[i].reshape(1, d), sc_m, sh_m, w_in, attn_q_gain[j], attn_k_gain[j],
                cos_t, sin_a, sin_b, seq_len)
            o = _dsa_attention(qt, k, vt, qit, ki, wit, batch, seq_len)
            xs = _out_proj(xs, o, attn_w_out[j].astype(BF16), g_m, seq_len)
        else:
            u_t = _norm_transposed(xs, norm_mix[i].reshape(1, d), sc_m, sh_m, seq_len)
            toep, bst, cout, lam = _ssm_prep(ssm_a_re[j], ssm_a_im[j], ssm_log_dt[j], ssm_b_re[j],
                                             ssm_b_im[j], ssm_c_re[j], ssm_c_im[j])
            g_t = _ssm_apply(u_t, toep, bst, cout, lam, ssm_d[j], seq_len)
            xs = _glu_residual(xs, g_t, ssm_w_glu[j].astype(BF16), g_m, seq_len)
        w_up = ffn_w_up[i].astype(BF16)
        xs = _conv_ffn(xs, norm_ffn[i].reshape(1, d), sc_f, sh_f, g_f, w_up[:, :D_FF], w_up[:, D_FF:],
                       ffn_conv_w[i], ffn_conv_b[i].reshape(1, -1), ffn_w_down[i].astype(BF16),
                       seq_len)
    return xs.reshape(batch, seq_len, d)
```

```python
import functools
import math

import jax
import jax.numpy as jnp
import numpy as np
from jax import lax
from jax.experimental import pallas as pl
from jax.experimental.pallas import tpu as pltpu

F32 = jnp.float32
BF16 = jnp.bfloat16
I32 = jnp.int32
I16 = jnp.int16

LANES = 128
SUBLANES = 8
VMEM_LIMIT = 56 << 20

D_MODEL = 1024
N_HEADS = 16
HEAD_DIM = 64
IDX_HEADS = 8
IDX_DIM = 64
CHUNK = 64
TOPK_MAX = 256
ROPE_THETA = 10000.0
SSM_GROUP = 16
N_GROUPS = D_MODEL // SSM_GROUP
SSM_STATE = 64
D_FF = 2816
EPS = 1e-6
NEG = -1e30
IN_COLS = 3 * D_MODEL + IDX_HEADS * IDX_DIM + IDX_DIM + IDX_HEADS
IN_COLS_PAD = 29 * LANES
SSM_T = 128


def _cparams(*sem):
    return pltpu.CompilerParams(dimension_semantics=sem, vmem_limit_bytes=VMEM_LIMIT)


def _sigmoid(x):
    return 1.0 / (1.0 + jnp.exp(-x))


def _norm_mod(x, g, sc, sh):
    ms = jnp.mean(x * x, axis=-1, keepdims=True)
    return (x * lax.rsqrt(ms + EPS) * g) * (1.0 + sc) + sh


def _bdot(a, b):
    return jnp.dot(a, b, preferred_element_type=F32)


def _mod_kernel(c_ref, w_ref, b_ref, o_ref):
    c = c_ref[...]
    cond = c * _sigmoid(c)
    o_ref[0] = _bdot(cond.astype(BF16), w_ref[0].astype(BF16)) + b_ref[0]


def _modulation(c, ada_w, ada_b):
    depth, d, six_d = ada_w.shape
    b = c.shape[0]
    c_pad = jnp.zeros((SUBLANES, d), F32).at[:b].set(c)
    out = pl.pallas_call(
        _mod_kernel,
        grid=(depth, six_d // d),
        in_specs=[
            pl.BlockSpec((SUBLANES, d), lambda i, j: (0, 0)),
            pl.BlockSpec((1, d, d), lambda i, j: (i, 0, j)),
            pl.BlockSpec((1, 1, d), lambda i, j: (i, 0, j)),
        ],
        out_specs=pl.BlockSpec((1, SUBLANES, d), lambda i, j: (i, 0, j)),
        out_shape=jax.ShapeDtypeStruct((depth, SUBLANES, six_d), F32),
        compiler_params=_cparams("parallel", "parallel"),
        name="adaln_mod",
    )(c_pad, ada_w, ada_b.reshape(depth, 1, six_d))
    return out[:, :b, :].reshape(depth, b, 6, 1, d)


def _vec_spec(rows_per_batch_tiles):
    return pl.BlockSpec((1, 1, D_MODEL), lambda i: (i // rows_per_batch_tiles, 0, 0))


def _const_spec(shape):
    nd = len(shape)
    return pl.BlockSpec(shape, lambda i: (0,) * nd)


FFN_TM = 512
FFN_FC = 256


def _ffn_kernel(x_ref, xp_ref, g_ref, sc_ref, sh_ref, gate_ref, wv_ref, wg_ref, cw_ref, cb_ref,
                wd_ref, o_ref, *, tiles_per_batch):
    i = pl.program_id(0)
    x = x_ref[...]
    g, sc, sh = g_ref[...], sc_ref[0], sh_ref[0]
    h = _norm_mod(x, g, sc, sh).astype(BF16)
    hp = _norm_mod(xp_ref[...], g, sc, sh).astype(BF16)
    keep_prev = (i % tiles_per_batch != 0).astype(F32)
    tm = x.shape[0]
    row8 = lax.broadcasted_iota(I32, (SUBLANES, FFN_FC), 0)

    def up_proj(w_ref, f):
        cs = slice(f * FFN_FC, (f + 1) * FFN_FC)
        return _bdot(h, w_ref[:, cs]), _bdot(hp, w_ref[:, cs]) * keep_prev

    def conv(ups, f, col0):
        up, upp = ups
        p7 = upp[7:8, :]
        p6 = upp[6:7, :]
        up1 = pltpu.roll(up, 1, 0)
        up2 = pltpu.roll(up, 2, 0)
        top1 = jnp.where(row8 == 0, p7, up1[:SUBLANES])
        top2 = jnp.where(row8 == 0, p6, jnp.where(row8 == 1, p7, up2[:SUBLANES]))
        up1 = jnp.concatenate([top1, up1[SUBLANES:]], axis=0)
        up2 = jnp.concatenate([top2, up2[SUBLANES:]], axis=0)
        wsl = slice(col0 + f * FFN_FC, col0 + (f + 1) * FFN_FC)
        cw = cw_ref[:, wsl]
        return up2 * cw[0:1] + up1 * cw[1:2] + up * cw[2:3] + cb_ref[:, wsl]

    acc = jnp.zeros((tm, D_MODEL), F32)
    n_f = D_FF // FFN_FC
    ups = (up_proj(wv_ref, 0), up_proj(wg_ref, 0))
    for f in range(n_f):
        cur = ups
        if f + 1 < n_f:
            ups = (up_proj(wv_ref, f + 1), up_proj(wg_ref, f + 1))
        val = conv(cur[0], f, 0)
        gt = conv(cur[1], f, D_FF)
        act = (gt * _sigmoid(gt) * val).astype(BF16)
        acc = acc + _bdot(act, wd_ref[f * FFN_FC:(f + 1) * FFN_FC, :])
    o_ref[...] = x + gate_ref[0] * acc


def _conv_ffn(x, norm_g, sc, sh, gate, w_val, w_gate, conv_w, conv_b, w_down, seq_len):
    n, d = x.shape
    tm = min(FFN_TM, seq_len)
    tpb = seq_len // tm
    kern = functools.partial(_ffn_kernel, tiles_per_batch=tpb)
    return pl.pallas_call(
        kern,
        grid=(n // tm,),
        in_specs=[
            pl.BlockSpec((tm, d), lambda i: (i, 0)),
            pl.BlockSpec((SUBLANES, d), lambda i: (jnp.maximum(i * (tm // SUBLANES) - 1, 0), 0)),
            _const_spec((1, d)),
            _vec_spec(tpb), _vec_spec(tpb), _vec_spec(tpb),
            _const_spec((d, D_FF)), _const_spec((d, D_FF)),
            _const_spec((3, 2 * D_FF)), _const_spec((1, 2 * D_FF)),
            _const_spec((D_FF, d)),
        ],
        out_specs=pl.BlockSpec((tm, d), lambda i: (i, 0)),
        out_shape=jax.ShapeDtypeStruct((n, d), F32),
        compiler_params=_cparams("parallel"),
        name="conv_ffn",
    )(x, x, norm_g, sc, sh, gate, w_val, w_gate, conv_w, conv_b, w_down)


ROPE_TM = 512


def _rope_kernel(pos_ref, inv_ref, cos_ref, sa_ref, sb_ref):
    ang = pos_ref[...].astype(F32) * inv_ref[...]
    cs = jnp.cos(ang)
    sn = jnp.sin(ang)
    lane = lax.broadcasted_iota(I32, ang.shape, 1)
    low = (lane % HEAD_DIM) < (HEAD_DIM // 2)
    cos_ref[...] = cs
    sa_ref[...] = jnp.where(low, -sn, 0.0)
    sb_ref[...] = jnp.where(low, 0.0, sn)


def _rope_tables(positions):
    n = positions.size
    tm = min(ROPE_TM, n)
    half = HEAD_DIM // 2
    inv = ROPE_THETA ** (-jnp.arange(half, dtype=F32) / half)
    inv = jnp.tile(inv, LANES // half).reshape(1, LANES)
    spec = pl.BlockSpec((tm, LANES), lambda i: (i, 0))
    return pl.pallas_call(
        _rope_kernel,
        grid=(n // tm,),
        in_specs=[pl.BlockSpec((tm, 1), lambda i: (i, 0)), _const_spec((1, LANES))],
        out_specs=[spec, spec, spec],
        out_shape=[jax.ShapeDtypeStruct((n, LANES), F32)] * 3,
        compiler_params=_cparams("parallel"),
        name="rope_tables",
    )(positions.reshape(n, 1), inv)


def _rope(x, cos_t, sin_a, sin_b):
    return (x * cos_t + pltpu.roll(x, LANES - HEAD_DIM // 2, 1) * sin_a
            + pltpu.roll(x, HEAD_DIM // 2, 1) * sin_b)


PROJ_TM = 256


def _attn_proj_kernel(x_ref, g_ref, sc_ref, sh_ref, w_ref, qg_ref, kg_ref, cos_ref, sa_ref, sb_ref,
                      hm_ref, qt_ref, k_ref, vt_ref, qit_ref, ki_ref, wit_ref):
    h = _norm_mod(x_ref[...], g_ref[...], sc_ref[0], sh_ref[0]).astype(BF16)
    cos_t, sin_a, sin_b = cos_ref[...], sa_ref[...], sb_ref[...]
    hm = hm_ref[...]
    n_qk = D_MODEL // LANES

    def group(gi):
        return _bdot(h, w_ref[:, gi * LANES:(gi + 1) * LANES])

    def head_norm(p, gain):
        sq = p * p
        hi = sq.astype(BF16)
        lo = (sq - hi.astype(F32)).astype(BF16)
        ms = _bdot(jnp.concatenate([hi, lo], axis=1), hm)
        return p * lax.rsqrt(ms + EPS) * gain

    for gi in range(n_qk):
        rows = slice(gi * LANES, (gi + 1) * LANES)
        q = _rope(head_norm(group(gi), qg_ref[...]), cos_t, sin_a, sin_b)
        qt_ref[rows, :] = (q * (HEAD_DIM ** -0.5 * LOG2E)).T.astype(BF16)
        k = _rope(head_norm(group(n_qk + gi), kg_ref[...]), cos_t, sin_a, sin_b)
        k_ref[gi] = k.astype(BF16)
        vt_ref[rows, :] = group(2 * n_qk + gi).T.astype(BF16)
    n_qi = IDX_HEADS * IDX_DIM // LANES
    for gi in range(n_qi):
        qi = _rope(group(3 * n_qk + gi), cos_t, sin_a, sin_b)
        qit_ref[gi * LANES:(gi + 1) * LANES, :] = qi.T.astype(BF16)
    tail = group(3 * n_qk + n_qi)
    ki = _rope(tail, cos_t, sin_a, sin_b)
    lane = lax.broadcasted_iota(I32, ki.shape, 1)
    ki_ref[...] = jnp.where(lane < IDX_DIM, ki, pltpu.roll(ki, IDX_DIM, 1)).astype(BF16)
    wi_t = (tail * (IDX_HEADS ** -0.5 * IDX_DIM ** -0.5)).T
    wit_ref[...] = wi_t[IDX_DIM:IDX_DIM + IDX_HEADS, :]


def _attn_proj(x, norm_g, sc, sh, w_in, q_gain, k_gain, cos_t, sin_a, sin_b, seq_len):
    n, d = x.shape
    tm = min(PROJ_TM, seq_len)
    tpb = seq_len // tm
    lane = jnp.arange(LANES)
    hm = ((lane[:, None] // HEAD_DIM) == (lane[None, :] // HEAD_DIM)).astype(BF16) / HEAD_DIM
    hm = jnp.concatenate([hm, hm], axis=0)
    qg2 = jnp.tile(q_gain, LANES // HEAD_DIM).reshape(1, LANES)
    kg2 = jnp.tile(k_gain, LANES // HEAD_DIM).reshape(1, LANES)
    row = lambda w: pl.BlockSpec((tm, w), lambda i: (i, 0))
    col = lambda h: pl.BlockSpec((h, tm), lambda i: (0, i))
    n_idx = IDX_HEADS * IDX_DIM
    return pl.pallas_call(
        _attn_proj_kernel,
        grid=(n // tm,),
        in_specs=[
            row(d), _const_spec((1, d)), _vec_spec(tpb), _vec_spec(tpb),
            _const_spec((d, IN_COLS_PAD)), _const_spec((1, LANES)), _const_spec((1, LANES)),
            row(LANES), row(LANES), row(LANES), _const_spec((2 * LANES, LANES)),
        ],
        out_specs=[col(d), pl.BlockSpec((d // LANES, tm, LANES), lambda i: (0, i, 0)), col(d),
                   col(n_idx), row(LANES), col(IDX_HEADS)],
        out_shape=[
            jax.ShapeDtypeStruct((d, n), BF16),
            jax.ShapeDtypeStruct((d // LANES, n, LANES), BF16),
            jax.ShapeDtypeStruct((d, n), BF16),
            jax.ShapeDtypeStruct((n_idx, n), BF16),
            jax.ShapeDtypeStruct((n, LANES), BF16),
            jax.ShapeDtypeStruct((IDX_HEADS, n), F32),
        ],
        compiler_params=_cparams("parallel"),
        name="attn_proj",
    )(x, norm_g, sc, sh, w_in, qg2, kg2, cos_t, sin_a, sin_b, hm)


ATT_QB = 256
ATT_TS = 512
ATT_RS = 64
ATT_SUM_ROWS = 16
LOG2E = math.log2(math.e)
I16_MIN = -(2 ** 15)


def _sort_key(s):
    bits = pltpu.bitcast(s + 0.0, I32)
    return bits ^ ((bits >> 31) & 0x7FFFFFFF)


def _key_halves(key):
    return (key >> 16).astype(I16), (key ^ 0x8000).astype(I16)


def _neg_key_halves():
    bits = int(np.float32(NEG).view(np.int32))
    key = bits ^ ((bits >> 31) & 0x7FFFFFFF)
    return key >> 16, (key & 0xFFFF) - 2 ** 15


NEG_KEY_HI, NEG_KEY_LO = _neg_key_halves()


def _attn_kernel(qt_ref, k_ref, vt_ref, qit_ref, ki_ref, wit_ref, o_ref, hi_ref, lo_ref, qm_ref,
                 sa_ref, sb_ref, *, seq_len, topk):
    qb = pl.program_id(1)
    hp = pl.program_id(2)
    nq, ts, rs = qt_ref.shape[1], ATT_TS, ATT_RS
    q0 = qb * nq
    n_tiles = 2 * ((q0 + nq + 2 * ts - 1) // (2 * ts))
    n_out = seq_len - n_tiles * ts
    sub = lax.broadcasted_iota(I32, (rs, nq), 0)

    @pl.when(hp == 0)
    def _select():
        row_head = lax.broadcasted_iota(I32, (LANES, nq), 0) // IDX_DIM
        for h in range(IDX_HEADS):
            pair = qit_ref[(h // 2) * LANES:(h // 2 + 1) * LANES, :]
            qm_ref[h] = jnp.where(row_head == (h % 2), pair, jnp.zeros_like(pair))
        q_chunk = (q0 + lax.broadcasted_iota(I32, (ts, nq), 1)) // CHUNK
        k_row = lax.broadcasted_iota(I32, (ts, nq), 0)

        def score_tile(masked, j, carry):
            r0 = pl.multiple_of(j * ts, ts)
            ki_t = ki_ref[pl.ds(r0, ts), :]
            s = jnp.zeros((ts, nq), F32)
            for h in range(IDX_HEADS):
                s = s + wit_ref[h:h + 1, :] * jnp.maximum(_bdot(ki_t, qm_ref[h]), 0.0)
            if masked:
                s = jnp.where(((r0 + k_row) // CHUNK) <= q_chunk, s, NEG)
            hi_ref[pl.ds(r0, ts), :], lo_ref[pl.ds(r0, ts), :] = _key_halves(_sort_key(s))
            return carry

        n_below = q0 // ts
        lax.fori_loop(0, n_below, functools.partial(score_tile, False), 0)
        lax.fori_loop(n_below, n_tiles, functools.partial(score_tile, True), 0)

        one16, zero16 = jnp.ones((rs, nq), I16), jnp.zeros((rs, nq), I16)

        def count16(ind_fn):
            def body(j, cnt):
                r0 = pl.multiple_of(j * ts, ts)
                for u in range(ts // rs):
                    cnt = cnt + ind_fn(r0 + u * rs)
                return cnt
            cnt = lax.fori_loop(0, n_tiles, body, zero16)
            tot = jnp.sum(cnt.astype(F32), axis=0, keepdims=True).astype(I32)
            return jnp.broadcast_to(tot, (rs, nq))

        def search16(ref, target, outside_fn, count_init):
            def cond(st):
                b, _, cnt_t = st
                return (b < 16) & (jnp.max((cnt_t - target).astype(F32)) > 0.0)

            def step(st):
                b, t, cnt_t = st
                bit = jnp.left_shift(jnp.int32(1), 15 - b)
                cand = jnp.where(b == 0, jnp.zeros_like(t), t | bit)
                c16 = cand.astype(I16)
                cnt = count16(lambda ru: jnp.where(ref[pl.ds(ru, rs), :] >= c16, one16, zero16))
                cnt = cnt + outside_fn(cand)
                take = cnt >= target
                return b + 1, jnp.where(take, cand, t), jnp.where(take, cnt, cnt_t)

            _, t, cnt_t = lax.while_loop(
                cond, step, (jnp.int32(0), jnp.full((rs, nq), I16_MIN, I32), count_init))
            return t, cnt_t

        k_arr = jnp.full((rs, nq), topk, I32)
        t_hi, cnt_hi = search16(hi_ref, k_arr, lambda c: jnp.where(c <= NEG_KEY_HI, n_out, 0),
                                jnp.full((rs, nq), seq_len, I32))
        t_hi16 = t_hi.astype(I16)
        n_above = (count16(lambda ru: jnp.where(hi_ref[pl.ds(ru, rs), :] > t_hi16, one16, zero16))
                   + jnp.where(t_hi < NEG_KEY_HI, n_out, 0))
        need_lo = topk - n_above
        out_match = jnp.where(t_hi == NEG_KEY_HI, n_out, 0)

        def mark_tile(j, carry):
            r0 = pl.multiple_of(j * ts, ts)
            for u in range(ts // rs):
                rows = pl.ds(r0 + u * rs, rs)
                lo_ref[rows, :] = jnp.where(hi_ref[rows, :] == t_hi16, lo_ref[rows, :],
                                            jnp.full((rs, nq), I16_MIN, I16))
            return carry

        lax.fori_loop(0, n_tiles, mark_tile, 0)
        t_lo, cnt_lo = search16(lo_ref, need_lo, lambda c: jnp.where(c <= NEG_KEY_LO, out_match, 0),
                                cnt_hi - n_above)
        t_lo16 = t_lo.astype(I16)
        sub16 = sub.astype(I16)

        def tie_cut():
            n_gt = (count16(lambda ru: jnp.where(lo_ref[pl.ds(ru, rs), :] > t_lo16, one16, zero16))
                    + jnp.where(t_lo < NEG_KEY_LO, out_match, 0))
            need = need_lo - n_gt
            n_eq = cnt_lo - n_gt
            real = jnp.where(t_hi > NEG_KEY_HI, 1,
                             jnp.where(t_hi == NEG_KEY_HI, jnp.where(t_lo > NEG_KEY_LO, 1, 0), 0))
            n_amb = jnp.sum((real * jnp.where(n_eq > need, 1, 0)).astype(F32))

            def search():
                def idx_step(b, c):
                    cand = c | jnp.left_shift(jnp.int32(1), 14 - b)
                    c16 = cand.astype(I16)

                    def ind(ru):
                        rows = pl.ds(ru, rs)
                        below = jnp.where((sub16 + ru.astype(I16)) < c16, one16, zero16)
                        tied = jnp.where(lo_ref[rows, :] == t_lo16, below, zero16)
                        return jnp.where(hi_ref[rows, :] == t_hi16, tied, zero16)

                    return jnp.where(count16(ind) <= need, cand, c)
                return real * lax.fori_loop(0, 15, idx_step, jnp.zeros((rs, nq), I32))

            return lax.cond(n_amb > 0.0, search, lambda: real * seq_len)

        exact_sets = jnp.max((cnt_lo - need_lo).astype(F32)) <= 0.0
        cut16 = lax.cond(exact_sets, lambda: jnp.full((rs, nq), seq_len, I32), tie_cut).astype(I16)
        zero_b, neg_b = jnp.zeros((rs, nq), BF16), jnp.full((rs, nq), NEG, BF16)

        def bias_tile(j, carry):
            r0 = pl.multiple_of(j * ts, ts)
            for u in range(ts // rs):
                ru = r0 + u * rs
                rows = pl.ds(ru, rs)
                hi, lo = hi_ref[rows, :], lo_ref[rows, :]
                tie_ok = jnp.where((sub16 + ru.astype(I16)) < cut16, zero_b, neg_b)
                at_hi = jnp.where(lo > t_lo16, zero_b, jnp.where(lo == t_lo16, tie_ok, neg_b))
                bias = jnp.where(hi > t_hi16, zero_b, jnp.where(hi == t_hi16, at_hi, neg_b))
                hi_ref[rows, :] = pltpu.bitcast(bias, I16)
            return carry

        lax.fori_loop(0, n_tiles, bias_tile, 0)

    qt = qt_ref[...]
    row_head = lax.broadcasted_iota(I32, (LANES, nq), 0) // HEAD_DIM
    qh = [jnp.where(row_head == hh, qt, jnp.zeros_like(qt)) for hh in range(2)]
    ones_rows = jnp.ones((ATT_SUM_ROWS, ts), BF16)

    def logits_stage(j, s_ref):
        r0 = pl.multiple_of(j * ts, ts)
        k_t = k_ref[pl.ds(r0, ts), :]
        bias = pltpu.bitcast(hi_ref[pl.ds(r0, ts), :], BF16).astype(F32)
        tile_max = []
        for hh in range(2):
            s = _bdot(k_t, qh[hh]) + bias
            s_ref[hh] = s
            tile_max.append(jnp.max(s, axis=0, keepdims=True))
        return tuple(tile_max)

    def value_stage(j, s_ref, tile_max, carry):
        r0 = pl.multiple_of(j * ts, ts)
        new = []
        for hh in range(2):
            m, acc = carry[hh]
            m_new = jnp.maximum(m, tile_max[hh])
            p = jnp.exp2(s_ref[hh] - m_new).astype(BF16)
            vt = vt_ref[hh * HEAD_DIM:(hh + 1) * HEAD_DIM, pl.ds(r0, ts)]
            acc = jnp.exp2(m - m_new) * acc + _bdot(jnp.concatenate([vt, ones_rows], axis=0), p)
            new.append((m_new, acc))
        return tuple(new)

    def tile_pair(last, i, state):
        max_a, carry = state
        max_b = logits_stage(2 * i + 1, sb_ref)
        carry = value_stage(2 * i, sa_ref, max_a, carry)
        if not last:
            max_a = logits_stage(2 * i + 2, sa_ref)
        carry = value_stage(2 * i + 1, sb_ref, max_b, carry)
        return max_a, carry

    head_init = (jnp.full((1, nq), NEG, F32), jnp.zeros((HEAD_DIM + ATT_SUM_ROWS, nq), F32))
    n_pairs = n_tiles // 2
    state = lax.fori_loop(0, n_pairs - 1, functools.partial(tile_pair, False),
                          (logits_stage(0, sa_ref), (head_init, head_init)))
    _, carry = tile_pair(True, n_pairs - 1, state)
    for hh in range(2):
        _, acc = carry[hh]
        o_ref[hh * HEAD_DIM:(hh + 1) * HEAD_DIM, :] = (
            acc[:HEAD_DIM] / acc[HEAD_DIM:HEAD_DIM + 1]).astype(BF16)


def _dsa_attention(qt, k, vt, qit, ki, wit, batch, seq_len):
    d, n = qt.shape
    nq = min(ATT_QB, seq_len)
    qpb = seq_len // nq
    n_idx = IDX_HEADS * IDX_DIM
    topk = min(TOPK_MAX, seq_len // 4)
    assert seq_len % (2 * ATT_TS) == 0 and seq_len % nq == 0
    kern = functools.partial(_attn_kernel, seq_len=seq_len, topk=topk)
    return pl.pallas_call(
        kern,
        grid=(batch, qpb, d // LANES),
        in_specs=[
            pl.BlockSpec((LANES, nq), lambda b, i, h: (h, b * qpb + i)),
            pl.BlockSpec((None, seq_len, LANES), lambda b, i, h: (h, b, 0)),
            pl.BlockSpec((LANES, seq_len), lambda b, i, h: (h, b)),
            pl.BlockSpec((n_idx, nq), lambda b, i, h: (0, b * qpb + i)),
            pl.BlockSpec((seq_len, LANES), lambda b, i, h: (b, 0)),
            pl.BlockSpec((IDX_HEADS, nq), lambda b, i, h: (0, b * qpb + i)),
        ],
        out_specs=pl.BlockSpec((LANES, nq), lambda b, i, h: (h, b * qpb + i)),
        out_shape=jax.ShapeDtypeStruct((d, n), BF16),
        scratch_shapes=[pltpu.VMEM((seq_len, nq), I16), pltpu.VMEM((seq_len, nq), I16),
                        pltpu.VMEM((IDX_HEADS, LANES, nq), BF16),
                        pltpu.VMEM((2, ATT_TS, nq), F32), pltpu.VMEM((2, ATT_TS, nq), F32)],
        compiler_params=_cparams("parallel", "parallel", "arbitrary"),
        name="dsa_attention",
    )(qt, k, vt, qit, ki, wit)


OUT_TM = 512


def _out_proj_kernel(x_ref, at_ref, w_ref, gate_ref, o_ref):
    y = lax.dot_general(at_ref[...], w_ref[...], (((0,), (0,)), ((), ())),
                        preferred_element_type=F32)
    o_ref[...] = x_ref[...] + gate_ref[0] * y


def _out_proj(x, a_t, w, gate, seq_len):
    n, d = x.shape
    tm = min(OUT_TM, seq_len)
    tpb = seq_len // tm
    row = pl.BlockSpec((tm, d), lambda i: (i, 0))
    return pl.pallas_call(
        _out_proj_kernel,
        grid=(n // tm,),
        in_specs=[row, pl.BlockSpec((d, tm), lambda i: (0, i)), _const_spec((d, d)), _vec_spec(tpb)],
        out_specs=row,
        out_shape=jax.ShapeDtypeStruct((n, d), F32),
        compiler_params=_cparams("parallel"),
        name="attn_out_proj",
    )(x, a_t, w, gate)


NORMT_TM = 512


def _norm_t_kernel(x_ref, g_ref, sc_ref, sh_ref, o_ref):
    o_ref[...] = _norm_mod(x_ref[...], g_ref[...], sc_ref[0], sh_ref[0]).T


def _norm_transposed(x, norm_g, sc, sh, seq_len):
    n, d = x.shape
    tm = min(NORMT_TM, seq_len)
    tpb = seq_len // tm
    return pl.pallas_call(
        _norm_t_kernel,
        grid=(n // tm,),
        in_specs=[pl.BlockSpec((tm, d), lambda i: (i, 0)), _const_spec((1, d)),
                  _vec_spec(tpb), _vec_spec(tpb)],
        out_specs=pl.BlockSpec((d, tm), lambda i: (0, i)),
        out_shape=jax.ShapeDtypeStruct((d, n), F32),
        compiler_params=_cparams("parallel"),
        name="ssm_norm_t",
    )(x, norm_g, sc, sh)


def _ssm_prep_kernel(are_ref, aim_ref, ldt_ref, brt_ref, bit_ref, cr_ref, ci_ref, crt_ref, cit_ref,
                     toep_ref, bst_ref, cout_ref, lam_ref, kt_ref):
    t_len, p, sg = SSM_T, SSM_STATE, SSM_GROUP
    a_re, a_im = are_ref[0], aim_ref[0]
    dt = jnp.exp(ldt_ref[0])
    ar, th = a_re * dt, a_im * dt
    decay = jnp.exp(ar)
    ab_re, ab_im = decay * jnp.cos(th), decay * jnp.sin(th)
    den = a_re * a_re + a_im * a_im
    nr, ni = ab_re - 1.0, ab_im
    coef_re = (nr * a_re + ni * a_im) / den
    coef_im = (ni * a_re - nr * a_im) / den
    brt, bit = brt_ref[0], bit_ref[0]
    bbr = coef_re * brt - coef_im * bit
    bbi = coef_re * bit + coef_im * brt
    cr, ci = cr_ref[0], ci_ref[0]
    crt, cit = crt_ref[0], cit_ref[0]

    def powers(nn):
        mag = jnp.exp(nn * ar)
        return mag * jnp.cos(nn * th), mag * jnp.sin(nn * th)

    n_iota = lax.broadcasted_iota(I32, (t_len, p), 0).astype(F32)
    l_re, l_im = powers(n_iota)
    l1_re, l1_im = powers(n_iota + 1.0)
    lr_re, lr_im = powers((t_len - 1.0) - n_iota)
    lcat_t = jnp.concatenate([l_re, l_im], axis=1).T
    l1cat_t = jnp.concatenate([l1_re, l1_im], axis=1).T
    l1_re_t, l1_im_t = l1cat_t[:p], l1cat_t[p:]

    mt = []
    for c in range(sg):
        m_re = cr[c:c + 1] * bbr - ci[c:c + 1] * bbi
        m_im = -(cr[c:c + 1] * bbi + ci[c:c + 1] * bbr)
        mt.append(jnp.concatenate([m_re, m_im], axis=1))
    mt = jnp.concatenate(mt, axis=0)
    kt_ref[...] = jnp.dot(mt, lcat_t, preferred_element_type=F32, precision=lax.Precision.HIGHEST)

    tau = lax.broadcasted_iota(I32, (t_len, t_len), 0)
    tt = lax.broadcasted_iota(I32, (t_len, t_len), 1)
    causal = tt >= tau

    def toep_column(c, carry):
        rows = kt_ref[pl.ds(pl.multiple_of(c * sg, sg), sg), :]
        for cp in range(sg):
            krow = jnp.broadcast_to(rows[cp:cp + 1, :], (t_len, t_len))
            tile = pltpu.roll(krow, 0, 1, stride=1, stride_axis=0)
            toep_ref[0, cp * t_len:(cp + 1) * t_len, pl.ds(pl.multiple_of(c * t_len, t_len), t_len)] = (
                jnp.where(causal, tile, 0.0).astype(BF16))
        return carry

    lax.fori_loop(0, sg, toep_column, 0)

    for c in range(sg):
        bst_ref[0, c * t_len:(c + 1) * t_len, :] = jnp.concatenate(
            [lr_re * bbr[c:c + 1] - lr_im * bbi[c:c + 1],
             lr_re * bbi[c:c + 1] + lr_im * bbr[c:c + 1]], axis=1).astype(BF16)
        ccr, cci = crt[:, c:c + 1], cit[:, c:c + 1]
        cout_ref[0, :, c * t_len:(c + 1) * t_len] = jnp.concatenate(
            [ccr * l1_re_t - cci * l1_im_t, -ccr * l1_im_t - cci * l1_re_t], axis=0).astype(BF16)

    lt_re, lt_im = powers(jnp.full((SUBLANES, p), float(t_len), F32))
    lam_ref[0] = jnp.concatenate([lt_re, lt_im], axis=1)


def _ssm_prep(a_re, a_im, log_dt, b_re, b_im, c_re, c_im):
    g, p, sg, t_len = N_GROUPS, SSM_STATE, SSM_GROUP, SSM_T
    vec = lambda a: a.reshape(g, 1, -1)
    spec3 = lambda s: pl.BlockSpec((1,) + s, lambda i: (i, 0, 0))
    return pl.pallas_call(
        _ssm_prep_kernel,
        grid=(g,),
        in_specs=[spec3((1, p)), spec3((1, p)), spec3((1, 1)), spec3((sg, p)), spec3((sg, p)),
                  spec3((sg, p)), spec3((sg, p)), spec3((p, sg)), spec3((p, sg))],
        out_specs=[spec3((sg * t_len, sg * t_len)), spec3((sg * t_len, 2 * p)),
                   spec3((2 * p, sg * t_len)), spec3((SUBLANES, 2 * p))],
        out_shape=[
            jax.ShapeDtypeStruct((g, sg * t_len, sg * t_len), BF16),
            jax.ShapeDtypeStruct((g, sg * t_len, 2 * p), BF16),
            jax.ShapeDtypeStruct((g, 2 * p, sg * t_len), BF16),
            jax.ShapeDtypeStruct((g, SUBLANES, 2 * p), F32),
        ],
        scratch_shapes=[pltpu.VMEM((sg * sg, t_len), F32)],
        compiler_params=_cparams("parallel"),
        name="ssm_prep",
    )(vec(a_re), vec(a_im), vec(log_dt), b_re.transpose(0, 2, 1), b_im.transpose(0, 2, 1),
      c_re, c_im, c_re.transpose(0, 2, 1), c_im.transpose(0, 2, 1))


def _gelu_tanh(x):
    return 0.5 * x * (1.0 + jnp.tanh(math.sqrt(2.0 / math.pi) * (x + 0.044715 * (x * x * x))))


def _ssm_kernel(u_ref, toep_ref, bst_ref, cout_ref, lam_ref, d_ref, o_ref, *, chunks_per_batch):
    sg, t_len, p = SSM_GROUP, SSM_T, SSM_STATE
    nc = u_ref.shape[1]
    lhs = jnp.concatenate([u_ref[c].astype(BF16) for c in range(sg)], axis=1)
    state = _bdot(lhs, bst_ref[0])
    row = lax.broadcasted_iota(I32, (nc, 2 * p), 0) % chunks_per_batch
    lane = lax.broadcasted_iota(I32, (1, 2 * p), 1)
    lam = lam_ref[0]
    a_re = lam[0:1, :p]
    a_im = lam[0:1, p:]

    def cmul_rows(a_re, a_im, x):
        a1 = jnp.concatenate([a_re, a_re], axis=1)
        a2 = jnp.concatenate([-a_im, a_im], axis=1)
        return a1 * x + a2 * pltpu.roll(x, p, 1)

    k = 1
    while k < chunks_per_batch:
        shifted = jnp.where(row >= k, pltpu.roll(state, k, 0), 0.0)
        state = state + cmul_rows(a_re, a_im, shifted)
        a_re, a_im = a_re * a_re - a_im * a_im, 2.0 * a_re * a_im
        k *= 2
    carry = jnp.where(row >= 1, pltpu.roll(state, 1, 0), 0.0)
    del lane
    y = _bdot(lhs, toep_ref[0]) + _bdot(carry.astype(BF16), cout_ref[0])
    for c in range(sg):
        yc = y[:, c * t_len:(c + 1) * t_len] + d_ref[0, c:c + 1, :] * u_ref[c]
        o_ref[c] = _gelu_tanh(yc).astype(BF16)


def _ssm_apply(u_t, toep, bst, cout, lam, d_skip, seq_len):
    d, n = u_t.shape
    g, sg, t_len, p = N_GROUPS, SSM_GROUP, SSM_T, SSM_STATE
    nc = n // t_len
    u3 = u_t.reshape(d, nc, t_len)
    d3 = jnp.broadcast_to(d_skip.reshape(g, sg, 1), (g, sg, t_len))
    kern = functools.partial(_ssm_kernel, chunks_per_batch=seq_len // t_len)
    spec3 = lambda s: pl.BlockSpec((1,) + s, lambda i: (i, 0, 0))
    blk = pl.BlockSpec((sg, nc, t_len), lambda i: (i, 0, 0))
    out = pl.pallas_call(
        kern,
        grid=(g,),
        in_specs=[blk, spec3((sg * t_len, sg * t_len)), spec3((sg * t_len, 2 * p)),
                  spec3((2 * p, sg * t_len)), spec3((SUBLANES, 2 * p)), spec3((sg, t_len))],
        out_specs=blk,
        out_shape=jax.ShapeDtypeStruct((d, nc, t_len), BF16),
        compiler_params=_cparams("parallel"),
        name="ssm_apply",
    )(u3, toep, bst, cout, lam, d3)
    return out.reshape(d, n)


GLU_TM = 512


def _glu_kernel(x_ref, gt_ref, w_ref, gate_ref, o_ref):
    z = lax.dot_general(gt_ref[...], w_ref[...], (((0,), (0,)), ((), ())),
                        preferred_element_type=F32)
    d = x_ref.shape[1]
    o_ref[...] = x_ref[...] + gate_ref[0] * (z[:, :d] * _sigmoid(z[:, d:]))


def _glu_residual(x, g_t, w_glu, gate, seq_len):
    n, d = x.shape
    tm = min(GLU_TM, seq_len)
    tpb = seq_len // tm
    row = pl.BlockSpec((tm, d), lambda i: (i, 0))
    return pl.pallas_call(
        _glu_kernel,
        grid=(n // tm,),
        in_specs=[row, pl.BlockSpec((d, tm), lambda i: (0, i)), _const_spec((d, 2 * d)),
                  _vec_spec(tpb)],
        out_specs=row,
        out_shape=jax.ShapeDtypeStruct((n, d), F32),
        compiler_params=_cparams("parallel"),
        name="ssm_glu",
    )(x, g_t, w_glu, gate)


def kernel(x, c, positions, ada_w, ada_b, norm_mix, norm_ffn, attn_w_in, attn_q_gain, attn_k_gain,
           attn_w_out, ssm_a_re, ssm_a_im, ssm_log_dt, ssm_b_re, ssm_b_im, ssm_c_re, ssm_c_im,
           ssm_d, ssm_w_glu, ffn_w_up, ffn_conv_w, ffn_conv_b, ffn_w_down):
    batch, seq_len, d = x.shape
    depth = ada_w.shape[0]
    n = batch * seq_len
    xs = x.reshape(n, d)
    mod = _modulation(c, ada_w, ada_b)
    cos_t, sin_a, sin_b = _rope_tables(positions)
    for i in range(depth):
        sh_m, sc_m, g_m, sh_f, sc_f, g_f = [mod[i, :, k] for k in range(6)]
        j = i // 2
        if i % 2 == 0:
            w_in = jnp.pad(attn_w_in[j], ((0, 0), (0, IN_COLS_PAD - IN_COLS))).astype(BF16)
            qt, k, vt, qit, ki, wit = _attn_proj(
                xs, norm_mix[i].reshape(1, d), sc_m, sh_m, w_in, attn_q_gain[j], attn_k_gain[j],
                cos_t, sin_a, sin_b, seq_len)
            o = _dsa_attention(qt, k, vt, qit, ki, wit, batch, seq_len)
            xs = _out_proj(xs, o, attn_w_out[j].astype(BF16), g_m, seq_len)
        else:
            u_t = _norm_transposed(xs, norm_mix[i].reshape(1, d), sc_m, sh_m, seq_len)
            toep, bst, cout, lam = _ssm_prep(ssm_a_re[j], ssm_a_im[j], ssm_log_dt[j], ssm_b_re[j],
                                             ssm_b_im[j], ssm_c_re[j], ssm_c_im[j])
            g_t = _ssm_apply(u_t, toep, bst, cout, lam, ssm_d[j], seq_len)
            xs = _glu_residual(xs, g_t, ssm_w_glu[j].astype(BF16), g_m, seq_len)
        w_up = ffn_w_up[i].astype(BF16)
        xs = _conv_ffn(xs, norm_ffn[i].reshape(1, d), sc_f, sh_f, g_f, w_up[:, :D_FF], w_up[:, D_FF:],
                       ffn_conv_w[i], ffn_conv_b[i].reshape(1, -1), ffn_w_down[i].astype(BF16),
                       seq_len)
    return xs.reshape(batch, seq_len, d)
```

```python
import functools
import math

import jax
import jax.numpy as jnp
import numpy as np
from jax import lax
from jax.experimental import pallas as pl
from jax.experimental.pallas import tpu as pltpu

F32 = jnp.float32
BF16 = jnp.bfloat16
I32 = jnp.int32
I16 = jnp.int16

LANES = 128
SUBLANES = 8
VMEM_LIMIT = 56 << 20

D_MODEL = 1024
N_HEADS = 16
HEAD_DIM = 64
IDX_HEADS = 8
IDX_DIM = 64
CHUNK = 64
TOPK_MAX = 256
ROPE_THETA = 10000.0
SSM_GROUP = 16
N_GROUPS = D_MODEL // SSM_GROUP
SSM_STATE = 64
D_FF = 2816
EPS = 1e-6
NEG = -1e30
IN_COLS = 3 * D_MODEL + IDX_HEADS * IDX_DIM + IDX_DIM + IDX_HEADS
IN_COLS_PAD = 29 * LANES
SSM_T = 128


def _cparams(*sem):
    return pltpu.CompilerParams(dimension_semantics=sem, vmem_limit_bytes=VMEM_LIMIT)


def _sigmoid(x):
    return 1.0 / (1.0 + jnp.exp(-x))


def _norm_mod(x, g, sc, sh):
    ms = jnp.mean(x * x, axis=-1, keepdims=True)
    return (x * lax.rsqrt(ms + EPS) * g) * (1.0 + sc) + sh


def _bdot(a, b):
    return jnp.dot(a, b, preferred_element_type=F32)


def _mod_kernel(c_ref, w_ref, b_ref, o_ref):
    c = c_ref[...]
    cond = c * _sigmoid(c)
    o_ref[0] = _bdot(cond.astype(BF16), w_ref[0].astype(BF16)) + b_ref[0]


def _modulation(c, ada_w, ada_b):
    depth, d, six_d = ada_w.shape
    b = c.shape[0]
    c_pad = jnp.zeros((SUBLANES, d), F32).at[:b].set(c)
    out = pl.pallas_call(
        _mod_kernel,
        grid=(depth, six_d // d),
        in_specs=[
            pl.BlockSpec((SUBLANES, d), lambda i, j: (0, 0)),
            pl.BlockSpec((1, d, d), lambda i, j: (i, 0, j)),
            pl.BlockSpec((1, 1, d), lambda i, j: (i, 0, j)),
        ],
        out_specs=pl.BlockSpec((1, SUBLANES, d), lambda i, j: (i, 0, j)),
        out_shape=jax.ShapeDtypeStruct((depth, SUBLANES, six_d), F32),
        compiler_params=_cparams("parallel", "parallel"),
        name="adaln_mod",
    )(c_pad, ada_w, ada_b.reshape(depth, 1, six_d))
    return out[:, :b, :].reshape(depth, b, 6, 1, d)


def _vec_spec(rows_per_batch_tiles):
    return pl.BlockSpec((1, 1, D_MODEL), lambda i: (i // rows_per_batch_tiles, 0, 0))


def _const_spec(shape):
    nd = len(shape)
    return pl.BlockSpec(shape, lambda i: (0,) * nd)


FFN_TM = 512
FFN_FC = 256


def _ffn_kernel(x_ref, xp_ref, g_ref, sc_ref, sh_ref, gate_ref, wv_ref, wg_ref, cw_ref, cb_ref,
                wd_ref, o_ref, *, tiles_per_batch):
    i = pl.program_id(0)
    x = x_ref[...]
    g, sc, sh = g_ref[...], sc_ref[0], sh_ref[0]
    h = _norm_mod(x, g, sc, sh).astype(BF16)
    hp = _norm_mod(xp_ref[...], g, sc, sh).astype(BF16)
    keep_prev = (i % tiles_per_batch != 0).astype(F32)
    tm = x.shape[0]
    row8 = lax.broadcasted_iota(I32, (SUBLANES, FFN_FC), 0)

    def up_proj(w_ref, f):
        cs = slice(f * FFN_FC, (f + 1) * FFN_FC)
        return _bdot(h, w_ref[:, cs]), _bdot(hp, w_ref[:, cs]) * keep_prev

    def conv(ups, f, col0):
        up, upp = ups
        p7 = upp[7:8, :]
        p6 = upp[6:7, :]
        up1 = pltpu.roll(up, 1, 0)
        up2 = pltpu.roll(up, 2, 0)
        top1 = jnp.where(row8 == 0, p7, up1[:SUBLANES])
        top2 = jnp.where(row8 == 0, p6, jnp.where(row8 == 1, p7, up2[:SUBLANES]))
        up1 = jnp.concatenate([top1, up1[SUBLANES:]], axis=0)
        up2 = jnp.concatenate([top2, up2[SUBLANES:]], axis=0)
        wsl = slice(col0 + f * FFN_FC, col0 + (f + 1) * FFN_FC)
        cw = cw_ref[:, wsl]
        return up2 * cw[0:1] + up1 * cw[1:2] + up * cw[2:3] + cb_ref[:, wsl]

    acc = jnp.zeros((tm, D_MODEL), F32)
    n_f = D_FF // FFN_FC
    ups = (up_proj(wv_ref, 0), up_proj(wg_ref, 0))
    for f in range(n_f):
        cur = ups
        if f + 1 < n_f:
            ups = (up_proj(wv_ref, f + 1), up_proj(wg_ref, f + 1))
        val = conv(cur[0], f, 0)
        gt = conv(cur[1], f, D_FF)
        act = (gt * _sigmoid(gt) * val).astype(BF16)
        acc = acc + _bdot(act, wd_ref[f * FFN_FC:(f + 1) * FFN_FC, :])
    o_ref[...] = x + gate_ref[0] * acc


def _conv_ffn(x, norm_g, sc, sh, gate, w_val, w_gate, conv_w, conv_b, w_down, seq_len):
    n, d = x.shape
    tm = min(FFN_TM, seq_len)
    tpb = seq_len // tm
    kern = functools.partial(_ffn_kernel, tiles_per_batch=tpb)
    return pl.pallas_call(
        kern,
        grid=(n // tm,),
        in_specs=[
            pl.BlockSpec((tm, d), lambda i: (i, 0)),
            pl.BlockSpec((SUBLANES, d), lambda i: (jnp.maximum(i * (tm // SUBLANES) - 1, 0), 0)),
            _const_spec((1, d)),
            _vec_spec(tpb), _vec_spec(tpb), _vec_spec(tpb),
            _const_spec((d, D_FF)), _const_spec((d, D_FF)),
            _const_spec((3, 2 * D_FF)), _const_spec((1, 2 * D_FF)),
            _const_spec((D_FF, d)),
        ],
        out_specs=pl.BlockSpec((tm, d), lambda i: (i, 0)),
        out_shape=jax.ShapeDtypeStruct((n, d), F32),
        compiler_params=_cparams("parallel"),
        name="conv_ffn",
    )(x, x, norm_g, sc, sh, gate, w_val, w_gate, conv_w, conv_b, w_down)


ROPE_TM = 512


def _rope_kernel(pos_ref, inv_ref, cos_ref, sa_ref, sb_ref):
    ang = pos_ref[...].astype(F32) * inv_ref[...]
    cs = jnp.cos(ang)
    sn = jnp.sin(ang)
    lane = lax.broadcasted_iota(I32, ang.shape, 1)
    low = (lane % HEAD_DIM) < (HEAD_DIM // 2)
    cos_ref[...] = cs
    sa_ref[...] = jnp.where(low, -sn, 0.0)
    sb_ref[...] = jnp.where(low, 0.0, sn)


def _rope_tables(positions):
    n = positions.size
    tm = min(ROPE_TM, n)
    half = HEAD_DIM // 2
    inv = ROPE_THETA ** (-jnp.arange(half, dtype=F32) / half)
    inv = jnp.tile(inv, LANES // half).reshape(1, LANES)
    spec = pl.BlockSpec((tm, LANES), lambda i: (i, 0))
    return pl.pallas_call(
        _rope_kernel,
        grid=(n // tm,),
        in_specs=[pl.BlockSpec((tm, 1), lambda i: (i, 0)), _const_spec((1, LANES))],
        out_specs=[spec, spec, spec],
        out_shape=[jax.ShapeDtypeStruct((n, LANES), F32)] * 3,
        compiler_params=_cparams("parallel"),
        name="rope_tables",
    )(positions.reshape(n, 1), inv)


def _rope(x, cos_t, sin_a, sin_b):
    return (x * cos_t + pltpu.roll(x, LANES - HEAD_DIM // 2, 1) * sin_a
            + pltpu.roll(x, HEAD_DIM // 2, 1) * sin_b)


PROJ_TM = 256


def _attn_proj_kernel(x_ref, g_ref, sc_ref, sh_ref, w_ref, qg_ref, kg_ref, cos_ref, sa_ref, sb_ref,
                      hm_ref, qt_ref, k_ref, vt_ref, qit_ref, ki_ref, wit_ref):
    h = _norm_mod(x_ref[...], g_ref[...], sc_ref[0], sh_ref[0]).astype(BF16)
    cos_t, sin_a, sin_b = cos_ref[...], sa_ref[...], sb_ref[...]
    hm = hm_ref[...]
    n_qk = D_MODEL // LANES

    def group(gi):
        return _bdot(h, w_ref[:, gi * LANES:(gi + 1) * LANES])

    def head_norm(p, gain):
        sq = p * p
        hi = sq.astype(BF16)
        lo = (sq - hi.astype(F32)).astype(BF16)
        ms = _bdot(jnp.concatenate([hi, lo], axis=1), hm)
        return p * lax.rsqrt(ms + EPS) * gain

    for gi in range(n_qk):
        rows = slice(gi * LANES, (gi + 1) * LANES)
        q = _rope(head_norm(group(gi), qg_ref[...]), cos_t, sin_a, sin_b)
        qt_ref[rows, :] = (q * (HEAD_DIM ** -0.5 * LOG2E)).T.astype(BF16)
        k = _rope(head_norm(group(n_qk + gi), kg_ref[...]), cos_t, sin_a, sin_b)
        k_ref[gi] = k.astype(BF16)
        vt_ref[rows, :] = group(2 * n_qk + gi).T.astype(BF16)
    n_qi = IDX_HEADS * IDX_DIM // LANES
    for gi in range(n_qi):
        qi = _rope(group(3 * n_qk + gi), cos_t, sin_a, sin_b)
        qit_ref[gi * LANES:(gi + 1) * LANES, :] = qi.T.astype(BF16)
    tail = group(3 * n_qk + n_qi)
    ki = _rope(tail, cos_t, sin_a, sin_b)
    lane = lax.broadcasted_iota(I32, ki.shape, 1)
    ki_ref[...] = jnp.where(lane < IDX_DIM, ki, pltpu.roll(ki, IDX_DIM, 1)).astype(BF16)
    wi_t = (tail * (IDX_HEADS ** -0.5 * IDX_DIM ** -0.5)).T
    wit_ref[...] = wi_t[IDX_DIM:IDX_DIM + IDX_HEADS, :]


def _attn_proj(x, norm_g, sc, sh, w_in, q_gain, k_gain, cos_t, sin_a, sin_b, seq_len):
    n, d = x.shape
    tm = min(PROJ_TM, seq_len)
    tpb = seq_len // tm
    lane = jnp.arange(LANES)
    hm = ((lane[:, None] // HEAD_DIM) == (lane[None, :] // HEAD_DIM)).astype(BF16) / HEAD_DIM
    hm = jnp.concatenate([hm, hm], axis=0)
    qg2 = jnp.tile(q_gain, LANES // HEAD_DIM).reshape(1, LANES)
    kg2 = jnp.tile(k_gain, LANES // HEAD_DIM).reshape(1, LANES)
    row = lambda w: pl.BlockSpec((tm, w), lambda i: (i, 0))
    col = lambda h: pl.BlockSpec((h, tm), lambda i: (0, i))
    n_idx = IDX_HEADS * IDX_DIM
    return pl.pallas_call(
        _attn_proj_kernel,
        grid=(n // tm,),
        in_specs=[
            row(d), _const_spec((1, d)), _vec_spec(tpb), _vec_spec(tpb),
            _const_spec((d, IN_COLS_PAD)), _const_spec((1, LANES)), _const_spec((1, LANES)),
            row(LANES), row(LANES), row(LANES), _const_spec((2 * LANES, LANES)),
        ],
        out_specs=[col(d), pl.BlockSpec((d // LANES, tm, LANES), lambda i: (0, i, 0)), col(d),
                   col(n_idx), row(LANES), col(IDX_HEADS)],
        out_shape=[
            jax.ShapeDtypeStruct((d, n), BF16),
            jax.ShapeDtypeStruct((d // LANES, n, LANES), BF16),
            jax.ShapeDtypeStruct((d, n), BF16),
            jax.ShapeDtypeStruct((n_idx, n), BF16),
            jax.ShapeDtypeStruct((n, LANES), BF16),
            jax.ShapeDtypeStruct((IDX_HEADS, n), F32),
        ],
        compiler_params=_cparams("parallel"),
        name="attn_proj",
    )(x, norm_g, sc, sh, w_in, qg2, kg2, cos_t, sin_a, sin_b, hm)


ATT_QB = 256
ATT_TS = 512
ATT_RS = 64
ATT_SUM_ROWS = 16
LOG2E = math.log2(math.e)
I16_MIN = -(2 ** 15)


def _sort_key(s):
    bits = pltpu.bitcast(s + 0.0, I32)
    return bits ^ ((bits >> 31) & 0x7FFFFFFF)


def _key_halves(key):
    return (key >> 16).astype(I16), (key ^ 0x8000).astype(I16)


def _neg_key_halves():
    bits = int(np.float32(NEG).view(np.int32))
    key = bits ^ ((bits >> 31) & 0x7FFFFFFF)
    return key >> 16, (key & 0xFFFF) - 2 ** 15


NEG_KEY_HI, NEG_KEY_LO = _neg_key_halves()


def _attn_kernel(qt_ref, k_ref, vt_ref, qit_ref, ki_ref, wit_ref, o_ref, hi_ref, lo_ref, qm_ref,
                 sa_ref, sb_ref, *, seq_len, topk):
    qb = pl.program_id(1)
    hp = pl.program_id(2)
    nq, ts, rs = qt_ref.shape[1], ATT_TS, ATT_RS
    q0 = qb * nq
    n_tiles = 2 * ((q0 + nq + 2 * ts - 1) // (2 * ts))
    n_out = seq_len - n_tiles * ts
    sub = lax.broadcasted_iota(I32, (rs, nq), 0)

    def select_keys(ts, n_tiles):
        row_head = lax.broadcasted_iota(I32, (LANES, nq), 0) // IDX_DIM
        for h in range(IDX_HEADS):
            pair = qit_ref[(h // 2) * LANES:(h // 2 + 1) * LANES, :]
            qm_ref[h] = jnp.where(row_head == (h % 2), pair, jnp.zeros_like(pair))
        q_chunk = (q0 + lax.broadcasted_iota(I32, (ts, nq), 1)) // CHUNK
        k_row = lax.broadcasted_iota(I32, (ts, nq), 0)

        def score_tile(masked, j, carry):
            r0 = pl.multiple_of(j * ts, ts)
            ki_t = ki_ref[pl.ds(r0, ts), :]
            s = jnp.zeros((ts, nq), F32)
            for h in range(IDX_HEADS):
                s = s + wit_ref[h:h + 1, :] * jnp.maximum(_bdot(ki_t, qm_ref[h]), 0.0)
            if masked:
                s = jnp.where(((r0 + k_row) // CHUNK) <= q_chunk, s, NEG)
            hi_ref[pl.ds(r0, ts), :], lo_ref[pl.ds(r0, ts), :] = _key_halves(_sort_key(s))
            return carry

        n_below = q0 // ts
        lax.fori_loop(0, n_below, functools.partial(score_tile, False), 0)
        lax.fori_loop(n_below, n_tiles, functools.partial(score_tile, True), 0)

        one16, zero16 = jnp.ones((rs, nq), I16), jnp.zeros((rs, nq), I16)

        def count16(ind_fn):
            def body(j, cnt):
                r0 = pl.multiple_of(j * ts, ts)
                for u in range(ts // rs):
                    cnt = cnt + ind_fn(r0 + u * rs)
                return cnt
            cnt = lax.fori_loop(0, n_tiles, body, zero16)
            tot = jnp.sum(cnt.astype(F32), axis=0, keepdims=True).astype(I32)
            return jnp.broadcast_to(tot, (rs, nq))

        def search16(ref, target, outside_fn, count_init, early_exit):
            def cond(st):
                b, _, cnt_t = st
                return (b < 16) & (jnp.max((cnt_t - target).astype(F32)) > 0.0)

            def step(st):
                b, t, cnt_t = st
                bit = jnp.left_shift(jnp.int32(1), 15 - b)
                cand = jnp.where(b == 0, jnp.zeros_like(t), t | bit)
                c16 = cand.astype(I16)
                cnt = count16(lambda ru: jnp.where(ref[pl.ds(ru, rs), :] >= c16, one16, zero16))
                cnt = cnt + outside_fn(cand)
                take = cnt >= target
                return b + 1, jnp.where(take, cand, t), jnp.where(take, cnt, cnt_t)

            init = (jnp.int32(0), jnp.full((rs, nq), I16_MIN, I32), count_init)
            if early_exit:
                _, t, cnt_t = lax.while_loop(cond, lambda st: step(step(st)), init)
            else:
                _, t, cnt_t = lax.fori_loop(0, 16, lambda _, st: step(st), init)
            return t, cnt_t

        k_arr = jnp.full((rs, nq), topk, I32)
        t_hi, cnt_hi = search16(hi_ref, k_arr, lambda c: jnp.where(c <= NEG_KEY_HI, n_out, 0),
                                jnp.full((rs, nq), seq_len, I32), early_exit=False)
        t_hi16 = t_hi.astype(I16)
        n_above = (count16(lambda ru: jnp.where(hi_ref[pl.ds(ru, rs), :] > t_hi16, one16, zero16))
                   + jnp.where(t_hi < NEG_KEY_HI, n_out, 0))
        need_lo = topk - n_above
        out_match = jnp.where(t_hi == NEG_KEY_HI, n_out, 0)

        def mark_tile(j, carry):
            r0 = pl.multiple_of(j * ts, ts)
            for u in range(ts // rs):
                rows = pl.ds(r0 + u * rs, rs)
                lo_ref[rows, :] = jnp.where(hi_ref[rows, :] == t_hi16, lo_ref[rows, :],
                                            jnp.full((rs, nq), I16_MIN, I16))
            return carry

        lax.fori_loop(0, n_tiles, mark_tile, 0)
        t_lo, cnt_lo = search16(lo_ref, need_lo, lambda c: jnp.where(c <= NEG_KEY_LO, out_match, 0),
                                cnt_hi - n_above, early_exit=True)
        t_lo16 = t_lo.astype(I16)
        sub16 = sub.astype(I16)

        def tie_cut():
            n_gt = (count16(lambda ru: jnp.where(lo_ref[pl.ds(ru, rs), :] > t_lo16, one16, zero16))
                    + jnp.where(t_lo < NEG_KEY_LO, out_match, 0))
            need = need_lo - n_gt
            n_eq = cnt_lo - n_gt
            real = jnp.where(t_hi > NEG_KEY_HI, 1,
                             jnp.where(t_hi == NEG_KEY_HI, jnp.where(t_lo > NEG_KEY_LO, 1, 0), 0))
            n_amb = jnp.sum((real * jnp.where(n_eq > need, 1, 0)).astype(F32))

            def search():
                def idx_step(b, c):
                    cand = c | jnp.left_shift(jnp.int32(1), 14 - b)
                    c16 = cand.astype(I16)

                    def ind(ru):
                        rows = pl.ds(ru, rs)
                        below = jnp.where((sub16 + ru.astype(I16)) < c16, one16, zero16)
                        tied = jnp.where(lo_ref[rows, :] == t_lo16, below, zero16)
                        return jnp.where(hi_ref[rows, :] == t_hi16, tied, zero16)

                    return jnp.where(count16(ind) <= need, cand, c)
                return real * lax.fori_loop(0, 15, idx_step, jnp.zeros((rs, nq), I32))

            return lax.cond(n_amb > 0.0, search, lambda: real * seq_len)

        exact_sets = jnp.max((cnt_lo - need_lo).astype(F32)) <= 0.0
        cut16 = lax.cond(exact_sets, lambda: jnp.full((rs, nq), seq_len, I32), tie_cut).astype(I16)
        zero_b, neg_b = jnp.zeros((rs, nq), BF16), jnp.full((rs, nq), NEG, BF16)

        def bias_tile(j, carry):
            r0 = pl.multiple_of(j * ts, ts)
            for u in range(ts // rs):
                ru = r0 + u * rs
                rows = pl.ds(ru, rs)
                hi, lo = hi_ref[rows, :], lo_ref[rows, :]
                tie_ok = jnp.where((sub16 + ru.astype(I16)) < cut16, zero_b, neg_b)
                at_hi = jnp.where(lo > t_lo16, zero_b, jnp.where(lo == t_lo16, tie_ok, neg_b))
                bias = jnp.where(hi > t_hi16, zero_b, jnp.where(hi == t_hi16, at_hi, neg_b))
                hi_ref[rows, :] = pltpu.bitcast(bias, I16)
            return carry

        lax.fori_loop(0, n_tiles, bias_tile, 0)

    @pl.when(hp == 0)
    def _select():
        select_keys(2 * ts, n_tiles // 2)

    qt = qt_ref[...]
    row_head = lax.broadcasted_iota(I32, (LANES, nq), 0) // HEAD_DIM
    qh = [jnp.where(row_head == hh, qt, jnp.zeros_like(qt)) for hh in range(2)]
    ones_rows = jnp.ones((ATT_SUM_ROWS, ts), BF16)

    def logits_stage(j, s_ref):
        r0 = pl.multiple_of(j * ts, ts)
        k_t = k_ref[pl.ds(r0, ts), :]
        bias = pltpu.bitcast(hi_ref[pl.ds(r0, ts), :], BF16).astype(F32)
        tile_max = []
        for hh in range(2):
            s = _bdot(k_t, qh[hh]) + bias
            s_ref[hh] = s
            tile_max.append(jnp.max(s, axis=0, keepdims=True))
        return tuple(tile_max)

    def value_stage(j, s_ref, tile_max, carry):
        r0 = pl.multiple_of(j * ts, ts)
        new = []
        for hh in range(2):
            m, acc = carry[hh]
            m_new = jnp.maximum(m, tile_max[hh])
            p = jnp.exp2(s_ref[hh] - m_new).astype(BF16)
            vt = vt_ref[hh * HEAD_DIM:(hh + 1) * HEAD_DIM, pl.ds(r0, ts)]
            acc = jnp.exp2(m - m_new) * acc + _bdot(jnp.concatenate([vt, ones_rows], axis=0), p)
            new.append((m_new, acc))
        return tuple(new)

    def tile_pair(last, i, state):
        max_a, carry = state
        max_b = logits_stage(2 * i + 1, sb_ref)
        carry = value_stage(2 * i, sa_ref, max_a, carry)
        if not last:
            max_a = logits_stage(2 * i + 2, sa_ref)
        carry = value_stage(2 * i + 1, sb_ref, max_b, carry)
        return max_a, carry

    head_init = (jnp.full((1, nq), NEG, F32), jnp.zeros((HEAD_DIM + ATT_SUM_ROWS, nq), F32))
    n_pairs = n_tiles // 2
    state = lax.fori_loop(0, n_pairs - 1, functools.partial(tile_pair, False),
                          (logits_stage(0, sa_ref), (head_init, head_init)))
    _, carry = tile_pair(True, n_pairs - 1, state)
    for hh in range(2):
        _, acc = carry[hh]
        o_ref[hh * HEAD_DIM:(hh + 1) * HEAD_DIM, :] = (
            acc[:HEAD_DIM] / acc[HEAD_DIM:HEAD_DIM + 1]).astype(BF16)


def _dsa_attention(qt, k, vt, qit, ki, wit, batch, seq_len):
    d, n = qt.shape
    nq = min(ATT_QB, seq_len)
    qpb = seq_len // nq
    n_idx = IDX_HEADS * IDX_DIM
    topk = min(TOPK_MAX, seq_len // 4)
    assert seq_len % (2 * ATT_TS) == 0 and seq_len % nq == 0
    kern = functools.partial(_attn_kernel, seq_len=seq_len, topk=topk)
    return pl.pallas_call(
        kern,
        grid=(batch, qpb, d // LANES),
        in_specs=[
            pl.BlockSpec((LANES, nq), lambda b, i, h: (h, b * qpb + i)),
            pl.BlockSpec((None, seq_len, LANES), lambda b, i, h: (h, b, 0)),
            pl.BlockSpec((LANES, seq_len), lambda b, i, h: (h, b)),
            pl.BlockSpec((n_idx, nq), lambda b, i, h: (0, b * qpb + i)),
            pl.BlockSpec((seq_len, LANES), lambda b, i, h: (b, 0)),
            pl.BlockSpec((IDX_HEADS, nq), lambda b, i, h: (0, b * qpb + i)),
        ],
        out_specs=pl.BlockSpec((LANES, nq), lambda b, i, h: (h, b * qpb + i)),
        out_shape=jax.ShapeDtypeStruct((d, n), BF16),
        scratch_shapes=[pltpu.VMEM((seq_len, nq), I16), pltpu.VMEM((seq_len, nq), I16),
                        pltpu.VMEM((IDX_HEADS, LANES, nq), BF16),
                        pltpu.VMEM((2, ATT_TS, nq), F32), pltpu.VMEM((2, ATT_TS, nq), F32)],
        compiler_params=_cparams("parallel", "parallel", "arbitrary"),
        name="dsa_attention",
    )(qt, k, vt, qit, ki, wit)


OUT_TM = 512


def _out_proj_kernel(x_ref, at_ref, w_ref, gate_ref, o_ref):
    y = lax.dot_general(at_ref[...], w_ref[...], (((0,), (0,)), ((), ())),
                        preferred_element_type=F32)
    o_ref[...] = x_ref[...] + gate_ref[0] * y


def _out_proj(x, a_t, w, gate, seq_len):
    n, d = x.shape
    tm = min(OUT_TM, seq_len)
    tpb = seq_len // tm
    row = pl.BlockSpec((tm, d), lambda i: (i, 0))
    return pl.pallas_call(
        _out_proj_kernel,
        grid=(n // tm,),
        in_specs=[row, pl.BlockSpec((d, tm), lambda i: (0, i)), _const_spec((d, d)), _vec_spec(tpb)],
        out_specs=row,
        out_shape=jax.ShapeDtypeStruct((n, d), F32),
        compiler_params=_cparams("parallel"),
        name="attn_out_proj",
    )(x, a_t, w, gate)


NORMT_TM = 512


def _norm_t_kernel(x_ref, g_ref, sc_ref, sh_ref, o_ref):
    o_ref[...] = _norm_mod(x_ref[...], g_ref[...], sc_ref[0], sh_ref[0]).T


def _norm_transposed(x, norm_g, sc, sh, seq_len):
    n, d = x.shape
    tm = min(NORMT_TM, seq_len)
    tpb = seq_len // tm
    return pl.pallas_call(
        _norm_t_kernel,
        grid=(n // tm,),
        in_specs=[pl.BlockSpec((tm, d), lambda i: (i, 0)), _const_spec((1, d)),
                  _vec_spec(tpb), _vec_spec(tpb)],
        out_specs=pl.BlockSpec((d, tm), lambda i: (0, i)),
        out_shape=jax.ShapeDtypeStruct((d, n), F32),
        compiler_params=_cparams("parallel"),
        name="ssm_norm_t",
    )(x, norm_g, sc, sh)


def _ssm_prep_kernel(are_ref, aim_ref, ldt_ref, brt_ref, bit_ref, cr_ref, ci_ref, crt_ref, cit_ref,
                     toep_ref, bst_ref, cout_ref, lam_ref, kt_ref):
    t_len, p, sg = SSM_T, SSM_STATE, SSM_GROUP
    a_re, a_im = are_ref[0], aim_ref[0]
    dt = jnp.exp(ldt_ref[0])
    ar, th = a_re * dt, a_im * dt
    decay = jnp.exp(ar)
    ab_re, ab_im = decay * jnp.cos(th), decay * jnp.sin(th)
    den = a_re * a_re + a_im * a_im
    nr, ni = ab_re - 1.0, ab_im
    coef_re = (nr * a_re + ni * a_im) / den
    coef_im = (ni * a_re - nr * a_im) / den
    brt, bit = brt_ref[0], bit_ref[0]
    bbr = coef_re * brt - coef_im * bit
    bbi = coef_re * bit + coef_im * brt
    cr, ci = cr_ref[0], ci_ref[0]
    crt, cit = crt_ref[0], cit_ref[0]

    def powers(nn):
        mag = jnp.exp(nn * ar)
        return mag * jnp.cos(nn * th), mag * jnp.sin(nn * th)

    n_iota = lax.broadcasted_iota(I32, (t_len, p), 0).astype(F32)
    l_re, l_im = powers(n_iota)
    l1_re, l1_im = powers(n_iota + 1.0)
    lr_re, lr_im = powers((t_len - 1.0) - n_iota)
    lcat_t = jnp.concatenate([l_re, l_im], axis=1).T
    l1cat_t = jnp.concatenate([l1_re, l1_im], axis=1).T
    l1_re_t, l1_im_t = l1cat_t[:p], l1cat_t[p:]

    mt = []
    for c in range(sg):
        m_re = cr[c:c + 1] * bbr - ci[c:c + 1] * bbi
        m_im = -(cr[c:c + 1] * bbi + ci[c:c + 1] * bbr)
        mt.append(jnp.concatenate([m_re, m_im], axis=1))
    mt = jnp.concatenate(mt, axis=0)
    kt_ref[...] = jnp.dot(mt, lcat_t, preferred_element_type=F32, precision=lax.Precision.HIGHEST)

    tau = lax.broadcasted_iota(I32, (t_len, t_len), 0)
    tt = lax.broadcasted_iota(I32, (t_len, t_len), 1)
    causal = tt >= tau

    def toep_column(c, carry):
        rows = kt_ref[pl.ds(pl.multiple_of(c * sg, sg), sg), :]
        for cp in range(sg):
            krow = jnp.broadcast_to(rows[cp:cp + 1, :], (t_len, t_len))
            tile = pltpu.roll(krow, 0, 1, stride=1, stride_axis=0)
            toep_ref[0, cp * t_len:(cp + 1) * t_len, pl.ds(pl.multiple_of(c * t_len, t_len), t_len)] = (
                jnp.where(causal, tile, 0.0).astype(BF16))
        return carry

    lax.fori_loop(0, sg, toep_column, 0)

    for c in range(sg):
        bst_ref[0, c * t_len:(c + 1) * t_len, :] = jnp.concatenate(
            [lr_re * bbr[c:c + 1] - lr_im * bbi[c:c + 1],
             lr_re * bbi[c:c + 1] + lr_im * bbr[c:c + 1]], axis=1).astype(BF16)
        ccr, cci = crt[:, c:c + 1], cit[:, c:c + 1]
        cout_ref[0, :, c * t_len:(c + 1) * t_len] = jnp.concatenate(
            [ccr * l1_re_t - cci * l1_im_t, -ccr * l1_im_t - cci * l1_re_t], axis=0).astype(BF16)

    lt_re, lt_im = powers(jnp.full((SUBLANES, p), float(t_len), F32))
    lam_ref[0] = jnp.concatenate([lt_re, lt_im], axis=1)


def _ssm_prep(a_re, a_im, log_dt, b_re, b_im, c_re, c_im):
    g, p, sg, t_len = N_GROUPS, SSM_STATE, SSM_GROUP, SSM_T
    vec = lambda a: a.reshape(g, 1, -1)
    spec3 = lambda s: pl.BlockSpec((1,) + s, lambda i: (i, 0, 0))
    return pl.pallas_call(
        _ssm_prep_kernel,
        grid=(g,),
        in_specs=[spec3((1, p)), spec3((1, p)), spec3((1, 1)), spec3((sg, p)), spec3((sg, p)),
                  spec3((sg, p)), spec3((sg, p)), spec3((p, sg)), spec3((p, sg))],
        out_specs=[spec3((sg * t_len, sg * t_len)), spec3((sg * t_len, 2 * p)),
                   spec3((2 * p, sg * t_len)), spec3((SUBLANES, 2 * p))],
        out_shape=[
            jax.ShapeDtypeStruct((g, sg * t_len, sg * t_len), BF16),
            jax.ShapeDtypeStruct((g, sg * t_len, 2 * p), BF16),
            jax.ShapeDtypeStruct((g, 2 * p, sg * t_len), BF16),
            jax.ShapeDtypeStruct((g, SUBLANES, 2 * p), F32),
        ],
        scratch_shapes=[pltpu.VMEM((sg * sg, t_len), F32)],
        compiler_params=_cparams("parallel"),
        name="ssm_prep",
    )(vec(a_re), vec(a_im), vec(log_dt), b_re.transpose(0, 2, 1), b_im.transpose(0, 2, 1),
      c_re, c_im, c_re.transpose(0, 2, 1), c_im.transpose(0, 2, 1))


def _gelu_tanh(x):
    return 0.5 * x * (1.0 + jnp.tanh(math.sqrt(2.0 / math.pi) * (x + 0.044715 * (x * x * x))))


def _ssm_kernel(u_ref, toep_ref, bst_ref, cout_ref, lam_ref, d_ref, o_ref, *, chunks_per_batch):
    sg, t_len, p = SSM_GROUP, SSM_T, SSM_STATE
    nc = u_ref.shape[1]
    lhs = jnp.concatenate([u_ref[c].astype(BF16) for c in range(sg)], axis=1)
    state = _bdot(lhs, bst_ref[0])
    row = lax.broadcasted_iota(I32, (nc, 2 * p), 0) % chunks_per_batch
    lane = lax.broadcasted_iota(I32, (1, 2 * p), 1)
    lam = lam_ref[0]
    a_re = lam[0:1, :p]
    a_im = lam[0:1, p:]

    def cmul_rows(a_re, a_im, x):
        a1 = jnp.concatenate([a_re, a_re], axis=1)
        a2 = jnp.concatenate([-a_im, a_im], axis=1)
        return a1 * x + a2 * pltpu.roll(x, p, 1)

    k = 1
    while k < chunks_per_batch:
        shifted = jnp.where(row >= k, pltpu.roll(state, k, 0), 0.0)
        state = state + cmul_rows(a_re, a_im, shifted)
        a_re, a_im = a_re * a_re - a_im * a_im, 2.0 * a_re * a_im
        k *= 2
    carry = jnp.where(row >= 1, pltpu.roll(state, 1, 0), 0.0)
    del lane
    y = _bdot(lhs, toep_ref[0]) + _bdot(carry.astype(BF16), cout_ref[0])
    for c in range(sg):
        yc = y[:, c * t_len:(c + 1) * t_len] + d_ref[0, c:c + 1, :] * u_ref[c]
        o_ref[c] = _gelu_tanh(yc).astype(BF16)


def _ssm_apply(u_t, toep, bst, cout, lam, d_skip, seq_len):
    d, n = u_t.shape
    g, sg, t_len, p = N_GROUPS, SSM_GROUP, SSM_T, SSM_STATE
    nc = n // t_len
    u3 = u_t.reshape(d, nc, t_len)
    d3 = jnp.broadcast_to(d_skip.reshape(g, sg, 1), (g, sg, t_len))
    kern = functools.partial(_ssm_kernel, chunks_per_batch=seq_len // t_len)
    spec3 = lambda s: pl.BlockSpec((1,) + s, lambda i: (i, 0, 0))
    blk = pl.BlockSpec((sg, nc, t_len), lambda i: (i, 0, 0))
    out = pl.pallas_call(
        kern,
        grid=(g,),
        in_specs=[blk, spec3((sg * t_len, sg * t_len)), spec3((sg * t_len, 2 * p)),
                  spec3((2 * p, sg * t_len)), spec3((SUBLANES, 2 * p)), spec3((sg, t_len))],
        out_specs=blk,
        out_shape=jax.ShapeDtypeStruct((d, nc, t_len), BF16),
        compiler_params=_cparams("parallel"),
        name="ssm_apply",
    )(u3, toep, bst, cout, lam, d3)
    return out.reshape(d, n)


GLU_TM = 512


def _glu_kernel(x_ref, gt_ref, w_ref, gate_ref, o_ref):
    z = lax.dot_general(gt_ref[...], w_ref[...], (((0,), (0,)), ((), ())),
                        preferred_element_type=F32)
    d = x_ref.shape[1]
    o_ref[...] = x_ref[...] + gate_ref[0] * (z[:, :d] * _sigmoid(z[:, d:]))


def _glu_residual(x, g_t, w_glu, gate, seq_len):
    n, d = x.shape
    tm = min(GLU_TM, seq_len)
    tpb = seq_len // tm
    row = pl.BlockSpec((tm, d), lambda i: (i, 0))
    return pl.pallas_call(
        _glu_kernel,
        grid=(n // tm,),
        in_specs=[row, pl.BlockSpec((d, tm), lambda i: (0, i)), _const_spec((d, 2 * d)),
                  _vec_spec(tpb)],
        out_specs=row,
        out_shape=jax.ShapeDtypeStruct((n, d), F32),
        compiler_params=_cparams("parallel"),
        name="ssm_glu",
    )(x, g_t, w_glu, gate)


def kernel(x, c, positions, ada_w, ada_b, norm_mix, norm_ffn, attn_w_in, attn_q_gain, attn_k_gain,
           attn_w_out, ssm_a_re, ssm_a_im, ssm_log_dt, ssm_b_re, ssm_b_im, ssm_c_re, ssm_c_im,
           ssm_d, ssm_w_glu, ffn_w_up, ffn_conv_w, ffn_conv_b, ffn_w_down):
    batch, seq_len, d = x.shape
    depth = ada_w.shape[0]
    n = batch * seq_len
    xs = x.reshape(n, d)
    mod = _modulation(c, ada_w, ada_b)
    cos_t, sin_a, sin_b = _rope_tables(positions)
    for i in range(depth):
        sh_m, sc_m, g_m, sh_f, sc_f, g_f = [mod[i, :, k] for k in range(6)]
        j = i // 2
        if i % 2 == 0:
            w_in = jnp.pad(attn_w_in[j], ((0, 0), (0, IN_COLS_PAD - IN_COLS))).astype(BF16)
            qt, k, vt, qit, ki, wit = _attn_proj(
                xs, norm_mix[i].reshape(1, d), sc_m, sh_m, w_in, attn_q_gain[j], attn_k_gain[j],
                cos_t, sin_a, sin_b, seq_len)
            o = _dsa_attention(qt, k, vt, qit, ki, wit, batch, seq_len)
            xs = _out_proj(xs, o, attn_w_out[j].astype(BF16), g_m, seq_len)
        else:
            u_t = _norm_transposed(xs, norm_mix[i].reshape(1, d), sc_m, sh_m, seq_len)
            toep, bst, cout, lam = _ssm_prep(ssm_a_re[j], ssm_a_im[j], ssm_log_dt[j], ssm_b_re[j],
                                             ssm_b_im[j], ssm_c_re[j], ssm_c_im[j])
            g_t = _ssm_apply(u_t, toep, bst, cout, lam, ssm_d[j], seq_len)
            xs = _glu_residual(xs, g_t, ssm_w_glu[j].astype(BF16), g_m, seq_len)
        w_up = ffn_w_up[i].astype(BF16)
        xs = _conv_ffn(xs, norm_ffn[i].reshape(1, d), sc_f, sh_f, g_f, w_up[:, :D_FF], w_up[:, D_FF:],
                       ffn_conv_w[i], ffn_conv_b[i].reshape(1, -1), ffn_w_down[i].astype(BF16),
                       seq_len)
    return xs.reshape(batch, seq_len, d)
```

```python
import functools
import math

import jax
import jax.numpy as jnp
import numpy as np
from jax import lax
from jax.experimental import pallas as pl
from jax.experimental.pallas import tpu as pltpu

F32 = jnp.float32
BF16 = jnp.bfloat16
I32 = jnp.int32
I16 = jnp.int16

LANES = 128
SUBLANES = 8
VMEM_LIMIT = 56 << 20

D_MODEL = 1024
N_HEADS = 16
HEAD_DIM = 64
IDX_HEADS = 8
IDX_DIM = 64
CHUNK = 64
TOPK_MAX = 256
ROPE_THETA = 10000.0
SSM_GROUP = 16
N_GROUPS = D_MODEL // SSM_GROUP
SSM_STATE = 64
D_FF = 2816
EPS = 1e-6
NEG = -1e30
IN_COLS = 3 * D_MODEL + IDX_HEADS * IDX_DIM + IDX_DIM + IDX_HEADS
IN_COLS_PAD = 29 * LANES
SSM_T = 128


def _cparams(*sem):
    return pltpu.CompilerParams(dimension_semantics=sem, vmem_limit_bytes=VMEM_LIMIT)


def _sigmoid(x):
    return 1.0 / (1.0 + jnp.exp(-x))


def _norm_mod(x, g, sc, sh):
    ms = jnp.mean(x * x, axis=-1, keepdims=True)
    return (x * lax.rsqrt(ms + EPS) * g) * (1.0 + sc) + sh


def _bdot(a, b):
    return jnp.dot(a, b, preferred_element_type=F32)


def _mod_kernel(c_ref, w_ref, b_ref, o_ref):
    c = c_ref[...]
    cond = c * _sigmoid(c)
    o_ref[0] = _bdot(cond.astype(BF16), w_ref[0].astype(BF16)) + b_ref[0]


def _modulation(c, ada_w, ada_b):
    depth, d, six_d = ada_w.shape
    b = c.shape[0]
    c_pad = jnp.zeros((SUBLANES, d), F32).at[:b].set(c)
    out = pl.pallas_call(
        _mod_kernel,
        grid=(depth, six_d // d),
        in_specs=[
            pl.BlockSpec((SUBLANES, d), lambda i, j: (0, 0)),
            pl.BlockSpec((1, d, d), lambda i, j: (i, 0, j)),
            pl.BlockSpec((1, 1, d), lambda i, j: (i, 0, j)),
        ],
        out_specs=pl.BlockSpec((1, SUBLANES, d), lambda i, j: (i, 0, j)),
        out_shape=jax.ShapeDtypeStruct((depth, SUBLANES, six_d), F32),
        compiler_params=_cparams("parallel", "parallel"),
        name="adaln_mod",
    )(c_pad, ada_w, ada_b.reshape(depth, 1, six_d))
    return out[:, :b, :].reshape(depth, b, 6, 1, d)


def _vec_spec(rows_per_batch_tiles):
    return pl.BlockSpec((1, 1, D_MODEL), lambda i: (i // rows_per_batch_tiles, 0, 0))


def _const_spec(shape):
    nd = len(shape)
    return pl.BlockSpec(shape, lambda i: (0,) * nd)


FFN_TM = 512
FFN_FC = 256


def _ffn_kernel(x_ref, xp_ref, g_ref, sc_ref, sh_ref, gate_ref, wv_ref, wg_ref, cw_ref, cb_ref,
                wd_ref, o_ref, *, tiles_per_batch):
    i = pl.program_id(0)
    x = x_ref[...]
    g, sc, sh = g_ref[...], sc_ref[0], sh_ref[0]
    h = _norm_mod(x, g, sc, sh).astype(BF16)
    hp = _norm_mod(xp_ref[...], g, sc, sh).astype(BF16)
    keep_prev = (i % tiles_per_batch != 0).astype(F32)
    tm = x.shape[0]
    row8 = lax.broadcasted_iota(I32, (SUBLANES, FFN_FC), 0)

    def up_proj(w_ref, f):
        cs = slice(f * FFN_FC, (f + 1) * FFN_FC)
        return _bdot(h, w_ref[:, cs]), _bdot(hp, w_ref[:, cs]) * keep_prev

    def conv(ups, f, col0):
        up, upp = ups
        p7 = upp[7:8, :]
        p6 = upp[6:7, :]
        up1 = pltpu.roll(up, 1, 0)
        up2 = pltpu.roll(up, 2, 0)
        top1 = jnp.where(row8 == 0, p7, up1[:SUBLANES])
        top2 = jnp.where(row8 == 0, p6, jnp.where(row8 == 1, p7, up2[:SUBLANES]))
        up1 = jnp.concatenate([top1, up1[SUBLANES:]], axis=0)
        up2 = jnp.concatenate([top2, up2[SUBLANES:]], axis=0)
        wsl = slice(col0 + f * FFN_FC, col0 + (f + 1) * FFN_FC)
        cw = cw_ref[:, wsl]
        return up2 * cw[0:1] + up1 * cw[1:2] + up * cw[2:3] + cb_ref[:, wsl]

    acc = jnp.zeros((tm, D_MODEL), F32)
    n_f = D_FF // FFN_FC
    ups = (up_proj(wv_ref, 0), up_proj(wg_ref, 0))
    for f in range(n_f):
        cur = ups
        if f + 1 < n_f:
            ups = (up_proj(wv_ref, f + 1), up_proj(wg_ref, f + 1))
        val = conv(cur[0], f, 0)
        gt = conv(cur[1], f, D_FF)
        act = (gt * _sigmoid(gt) * val).astype(BF16)
        acc = acc + _bdot(act, wd_ref[f * FFN_FC:(f + 1) * FFN_FC, :])
    o_ref[...] = x + gate_ref[0] * acc


def _conv_ffn(x, norm_g, sc, sh, gate, w_val, w_gate, conv_w, conv_b, w_down, seq_len):
    n, d = x.shape
    tm = min(FFN_TM, seq_len)
    tpb = seq_len // tm
    kern = functools.partial(_ffn_kernel, tiles_per_batch=tpb)
    return pl.pallas_call(
        kern,
        grid=(n // tm,),
        in_specs=[
            pl.BlockSpec((tm, d), lambda i: (i, 0)),
            pl.BlockSpec((SUBLANES, d), lambda i: (jnp.maximum(i * (tm // SUBLANES) - 1, 0), 0)),
            _const_spec((1, d)),
            _vec_spec(tpb), _vec_spec(tpb), _vec_spec(tpb),
            _const_spec((d, D_FF)), _const_spec((d, D_FF)),
            _const_spec((3, 2 * D_FF)), _const_spec((1, 2 * D_FF)),
            _const_spec((D_FF, d)),
        ],
        out_specs=pl.BlockSpec((tm, d), lambda i: (i, 0)),
        out_shape=jax.ShapeDtypeStruct((n, d), F32),
        compiler_params=_cparams("parallel"),
        name="conv_ffn",
    )(x, x, norm_g, sc, sh, gate, w_val, w_gate, conv_w, conv_b, w_down)


ROPE_TM = 512


def _rope_kernel(pos_ref, inv_ref, cos_ref, sa_ref, sb_ref):
    ang = pos_ref[...].astype(F32) * inv_ref[...]
    cs = jnp.cos(ang)
    sn = jnp.sin(ang)
    lane = lax.broadcasted_iota(I32, ang.shape, 1)
    low = (lane % HEAD_DIM) < (HEAD_DIM // 2)
    cos_ref[...] = cs
    sa_ref[...] = jnp.where(low, -sn, 0.0)
    sb_ref[...] = jnp.where(low, 0.0, sn)


def _rope_tables(positions):
    n = positions.size
    tm = min(ROPE_TM, n)
    half = HEAD_DIM // 2
    inv = ROPE_THETA ** (-jnp.arange(half, dtype=F32) / half)
    inv = jnp.tile(inv, LANES // half).reshape(1, LANES)
    spec = pl.BlockSpec((tm, LANES), lambda i: (i, 0))
    return pl.pallas_call(
        _rope_kernel,
        grid=(n // tm,),
        in_specs=[pl.BlockSpec((tm, 1), lambda i: (i, 0)), _const_spec((1, LANES))],
        out_specs=[spec, spec, spec],
        out_shape=[jax.ShapeDtypeStruct((n, LANES), F32)] * 3,
        compiler_params=_cparams("parallel"),
        name="rope_tables",
    )(positions.reshape(n, 1), inv)


def _rope(x, cos_t, sin_a, sin_b):
    return (x * cos_t + pltpu.roll(x, LANES - HEAD_DIM // 2, 1) * sin_a
            + pltpu.roll(x, HEAD_DIM // 2, 1) * sin_b)


PROJ_TM = 256


def _attn_proj_kernel(x_ref, g_ref, sc_ref, sh_ref, w_ref, qg_ref, kg_ref, cos_ref, sa_ref, sb_ref,
                      hm_ref, qt_ref, k_ref, vt_ref, qit_ref, ki_ref, wit_ref):
    h = _norm_mod(x_ref[...], g_ref[...], sc_ref[0], sh_ref[0]).astype(BF16)
    cos_t, sin_a, sin_b = cos_ref[...], sa_ref[...], sb_ref[...]
    hm = hm_ref[...]
    n_qk = D_MODEL // LANES

    def group(gi):
        return _bdot(h, w_ref[:, gi * LANES:(gi + 1) * LANES])

    def head_norm(p, gain):
        sq = p * p
        hi = sq.astype(BF16)
        lo = (sq - hi.astype(F32)).astype(BF16)
        ms = _bdot(jnp.concatenate([hi, lo], axis=1), hm)
        return p * lax.rsqrt(ms + EPS) * gain

    for gi in range(n_qk):
        rows = slice(gi * LANES, (gi + 1) * LANES)
        q = _rope(head_norm(group(gi), qg_ref[...]), cos_t, sin_a, sin_b)
        qt_ref[rows, :] = (q * (HEAD_DIM ** -0.5 * LOG2E)).T.astype(BF16)
        k = _rope(head_norm(group(n_qk + gi), kg_ref[...]), cos_t, sin_a, sin_b)
        k_ref[gi] = k.astype(BF16)
        vt_ref[rows, :] = group(2 * n_qk + gi).T.astype(BF16)
    n_qi = IDX_HEADS * IDX_DIM // LANES
    for gi in range(n_qi):
        qi = _rope(group(3 * n_qk + gi), cos_t, sin_a, sin_b)
        qit_ref[gi * LANES:(gi + 1) * LANES, :] = qi.T.astype(BF16)
    tail = group(3 * n_qk + n_qi)
    ki = _rope(tail, cos_t, sin_a, sin_b)
    lane = lax.broadcasted_iota(I32, ki.shape, 1)
    ki_ref[...] = jnp.where(lane < IDX_DIM, ki, pltpu.roll(ki, IDX_DIM, 1)).astype(BF16)
    wi_t = (tail * (IDX_HEADS ** -0.5 * IDX_DIM ** -0.5)).T
    wit_ref[...] = wi_t[IDX_DIM:IDX_DIM + IDX_HEADS, :]


def _attn_proj(x, norm_g, sc, sh, w_in, q_gain, k_gain, cos_t, sin_a, sin_b, seq_len):
    n, d = x.shape
    tm = min(PROJ_TM, seq_len)
    tpb = seq_len // tm
    lane = jnp.arange(LANES)
    hm = ((lane[:, None] // HEAD_DIM) == (lane[None, :] // HEAD_DIM)).astype(BF16) / HEAD_DIM
    hm = jnp.concatenate([hm, hm], axis=0)
    qg2 = jnp.tile(q_gain, LANES // HEAD_DIM).reshape(1, LANES)
    kg2 = jnp.tile(k_gain, LANES // HEAD_DIM).reshape(1, LANES)
    row = lambda w: pl.BlockSpec((tm, w), lambda i: (i, 0))
    col = lambda h: pl.BlockSpec((h, tm), lambda i: (0, i))
    n_idx = IDX_HEADS * IDX_DIM
    return pl.pallas_call(
        _attn_proj_kernel,
        grid=(n // tm,),
        in_specs=[
            row(d), _const_spec((1, d)), _vec_spec(tpb), _vec_spec(tpb),
            _const_spec((d, IN_COLS_PAD)), _const_spec((1, LANES)), _const_spec((1, LANES)),
            row(LANES), row(LANES), row(LANES), _const_spec((2 * LANES, LANES)),
        ],
        out_specs=[col(d), pl.BlockSpec((d // LANES, tm, LANES), lambda i: (0, i, 0)), col(d),
                   col(n_idx), row(LANES), col(IDX_HEADS)],
        out_shape=[
            jax.ShapeDtypeStruct((d, n), BF16),
            jax.ShapeDtypeStruct((d // LANES, n, LANES), BF16),
            jax.ShapeDtypeStruct((d, n), BF16),
            jax.ShapeDtypeStruct((n_idx, n), BF16),
            jax.ShapeDtypeStruct((n, LANES), BF16),
            jax.ShapeDtypeStruct((IDX_HEADS, n), F32),
        ],
        compiler_params=_cparams("parallel"),
        name="attn_proj",
    )(x, norm_g, sc, sh, w_in, qg2, kg2, cos_t, sin_a, sin_b, hm)


ATT_QB = 256
ATT_TS = 512
ATT_RS = 64
ATT_SUM_ROWS = 16
LOG2E = math.log2(math.e)
I16_MIN = -(2 ** 15)


def _sort_key(s):
    bits = pltpu.bitcast(s + 0.0, I32)
    return bits ^ ((bits >> 31) & 0x7FFFFFFF)


def _key_halves(key):
    return (key >> 16).astype(I16), (key ^ 0x8000).astype(I16)


def _neg_key_halves():
    bits = int(np.float32(NEG).view(np.int32))
    key = bits ^ ((bits >> 31) & 0x7FFFFFFF)
    return key >> 16, (key & 0xFFFF) - 2 ** 15


NEG_KEY_HI, NEG_KEY_LO = _neg_key_halves()


def _attn_kernel(qt_ref, k_ref, vt_ref, qit_ref, ki_ref, wit_ref, o_ref, hi_ref, lo_ref, qm_ref,
                 sa_ref, sb_ref, *, seq_len, topk):
    qb = pl.program_id(1)
    hp = pl.program_id(2)
    nq, ts, rs = qt_ref.shape[1], ATT_TS, ATT_RS
    q0 = qb * nq
    n_tiles = 2 * ((q0 + nq + 2 * ts - 1) // (2 * ts))
    n_out = seq_len - n_tiles * ts
    sub = lax.broadcasted_iota(I32, (rs, nq), 0)

    def select_keys(ts, n_tiles):
        row_head = lax.broadcasted_iota(I32, (LANES, nq), 0) // IDX_DIM
        for h in range(IDX_HEADS):
            pair = qit_ref[(h // 2) * LANES:(h // 2 + 1) * LANES, :]
            qm_ref[h] = jnp.where(row_head == (h % 2), pair, jnp.zeros_like(pair))
        q_chunk = (q0 + lax.broadcasted_iota(I32, (ts, nq), 1)) // CHUNK
        k_row = lax.broadcasted_iota(I32, (ts, nq), 0)

        def score_tile(masked, j, carry):
            r0 = pl.multiple_of(j * ts, ts)
            ki_t = ki_ref[pl.ds(r0, ts), :]
            s = jnp.zeros((ts, nq), F32)
            for h in range(IDX_HEADS):
                s = s + wit_ref[h:h + 1, :] * jnp.maximum(_bdot(ki_t, qm_ref[h]), 0.0)
            if masked:
                s = jnp.where(((r0 + k_row) // CHUNK) <= q_chunk, s, NEG)
            hi_ref[pl.ds(r0, ts), :], lo_ref[pl.ds(r0, ts), :] = _key_halves(_sort_key(s))
            return carry

        n_below = q0 // ts
        lax.fori_loop(0, n_below, functools.partial(score_tile, False), 0)
        lax.fori_loop(n_below, n_tiles, functools.partial(score_tile, True), 0)

        one16, zero16 = jnp.ones((rs, nq), I16), jnp.zeros((rs, nq), I16)

        def count16(ind_fn):
            def body(j, cnt):
                r0 = pl.multiple_of(j * ts, ts)
                for u in range(ts // rs):
                    cnt = cnt + ind_fn(r0 + u * rs)
                return cnt
            cnt = lax.fori_loop(0, n_tiles, body, zero16)
            tot = jnp.sum(cnt.astype(F32), axis=0, keepdims=True).astype(I32)
            return jnp.broadcast_to(tot, (rs, nq))

        def search16(ref, target, outside_fn, count_init, early_exit):
            def cond(st):
                b, _, cnt_t = st
                return (b < 16) & (jnp.max((cnt_t - target).astype(F32)) > 0.0)

            def step(st):
                b, t, cnt_t = st
                bit = jnp.left_shift(jnp.int32(1), 15 - b)
                cand = jnp.where(b == 0, jnp.zeros_like(t), t | bit)
                c16 = cand.astype(I16)
                cnt = count16(lambda ru: jnp.where(ref[pl.ds(ru, rs), :] >= c16, one16, zero16))
                cnt = cnt + outside_fn(cand)
                take = cnt >= target
                return b + 1, jnp.where(take, cand, t), jnp.where(take, cnt, cnt_t)

            init = (jnp.int32(0), jnp.full((rs, nq), I16_MIN, I32), count_init)
            if early_exit:
                _, t, cnt_t = lax.while_loop(cond, lambda st: step(step(st)), init)
            else:
                _, t, cnt_t = lax.fori_loop(0, 16, lambda _, st: step(st), init)
            return t, cnt_t

        k_arr = jnp.full((rs, nq), topk, I32)
        t_hi, cnt_hi = search16(hi_ref, k_arr, lambda c: jnp.where(c <= NEG_KEY_HI, n_out, 0),
                                jnp.full((rs, nq), seq_len, I32), early_exit=False)
        t_hi16 = t_hi.astype(I16)
        n_above = (count16(lambda ru: jnp.where(hi_ref[pl.ds(ru, rs), :] > t_hi16, one16, zero16))
                   + jnp.where(t_hi < NEG_KEY_HI, n_out, 0))
        need_lo = topk - n_above
        out_match = jnp.where(t_hi == NEG_KEY_HI, n_out, 0)

        def mark_tile(j, carry):
            r0 = pl.multiple_of(j * ts, ts)
            for u in range(ts // rs):
                rows = pl.ds(r0 + u * rs, rs)
                lo_ref[rows, :] = jnp.where(hi_ref[rows, :] == t_hi16, lo_ref[rows, :],
                                            jnp.full((rs, nq), I16_MIN, I16))
            return carry

        lax.fori_loop(0, n_tiles, mark_tile, 0)
        t_lo, cnt_lo = search16(lo_ref, need_lo, lambda c: jnp.where(c <= NEG_KEY_LO, out_match, 0),
                                cnt_hi - n_above, early_exit=True)
        t_lo16 = t_lo.astype(I16)
        sub16 = sub.astype(I16)

        def tie_cut():
            n_gt = (count16(lambda ru: jnp.where(lo_ref[pl.ds(ru, rs), :] > t_lo16, one16, zero16))
                    + jnp.where(t_lo < NEG_KEY_LO, out_match, 0))
            need = need_lo - n_gt
            n_eq = cnt_lo - n_gt
            real = jnp.where(t_hi > NEG_KEY_HI, 1,
                             jnp.where(t_hi == NEG_KEY_HI, jnp.where(t_lo > NEG_KEY_LO, 1, 0), 0))
            n_amb = jnp.sum((real * jnp.where(n_eq > need, 1, 0)).astype(F32))

            def search():
                def idx_step(b, c):
                    cand = c | jnp.left_shift(jnp.int32(1), 14 - b)
                    c16 = cand.astype(I16)

                    def ind(ru):
                        rows = pl.ds(ru, rs)
                        below = jnp.where((sub16 + ru.astype(I16)) < c16, one16, zero16)
                        tied = jnp.where(lo_ref[rows, :] == t_lo16, below, zero16)
                        return jnp.where(hi_ref[rows, :] == t_hi16, tied, zero16)

                    return jnp.where(count16(ind) <= need, cand, c)
                return real * lax.fori_loop(0, 15, idx_step, jnp.zeros((rs, nq), I32))

            return lax.cond(n_amb > 0.0, search, lambda: real * seq_len)

        exact_sets = jnp.max((cnt_lo - need_lo).astype(F32)) <= 0.0
        cut16 = lax.cond(exact_sets, lambda: jnp.full((rs, nq), seq_len, I32), tie_cut).astype(I16)
        zero_b, neg_b = jnp.zeros((rs, nq), BF16), jnp.full((rs, nq), NEG, BF16)

        def bias_tile(j, carry):
            r0 = pl.multiple_of(j * ts, ts)
            for u in range(ts // rs):
                ru = r0 + u * rs
                rows = pl.ds(ru, rs)
                hi, lo = hi_ref[rows, :], lo_ref[rows, :]
                tie_ok = jnp.where((sub16 + ru.astype(I16)) < cut16, zero_b, neg_b)
                at_hi = jnp.where(lo > t_lo16, zero_b, jnp.where(lo == t_lo16, tie_ok, neg_b))
                bias = jnp.where(hi > t_hi16, zero_b, jnp.where(hi == t_hi16, at_hi, neg_b))
                hi_ref[rows, :] = pltpu.bitcast(bias, I16)
            return carry

        lax.fori_loop(0, n_tiles, bias_tile, 0)

    @pl.when(hp == 0)
    def _select():
        select_keys(2 * ts, n_tiles // 2)

    qt = qt_ref[...]
    row_head = lax.broadcasted_iota(I32, (LANES, nq), 0) // HEAD_DIM
    qh = [jnp.where(row_head == hh, qt, jnp.zeros_like(qt)) for hh in range(2)]
    ones_rows = jnp.ones((ATT_SUM_ROWS, ts), BF16)

    def logits_stage(j, s_ref):
        r0 = pl.multiple_of(j * ts, ts)
        k_t = k_ref[pl.ds(r0, ts), :]
        bias = pltpu.bitcast(hi_ref[pl.ds(r0, ts), :], BF16).astype(F32)
        tile_max = []
        for hh in range(2):
            s = _bdot(k_t, qh[hh]) + bias
            s_ref[hh] = s
            tile_max.append(jnp.max(s, axis=0, keepdims=True))
        return tuple(tile_max)

    def value_stage(j, s_ref, tile_max, carry):
        r0 = pl.multiple_of(j * ts, ts)
        new = []
        for hh in range(2):
            m, acc = carry[hh]
            m_new = jnp.maximum(m, tile_max[hh])
            p = jnp.exp2(s_ref[hh] - m_new).astype(BF16)
            vt = vt_ref[hh * HEAD_DIM:(hh + 1) * HEAD_DIM, pl.ds(r0, ts)]
            acc = jnp.exp2(m - m_new) * acc + _bdot(jnp.concatenate([vt, ones_rows], axis=0), p)
            new.append((m_new, acc))
        return tuple(new)

    def tile_pair(last, i, state):
        max_a, carry = state
        max_b = logits_stage(2 * i + 1, sb_ref)
        carry = value_stage(2 * i, sa_ref, max_a, carry)
        if not last:
            max_a = logits_stage(2 * i + 2, sa_ref)
        carry = value_stage(2 * i + 1, sb_ref, max_b, carry)
        return max_a, carry

    def tile_quad(last, i, state):
        state = tile_pair(False, 2 * i, state)
        return tile_pair(last, 2 * i + 1, state)

    head_init = (jnp.full((1, nq), NEG, F32), jnp.zeros((HEAD_DIM + ATT_SUM_ROWS, nq), F32))
    n_quads = n_tiles // 4
    odd_pair = (n_tiles // 2) % 2
    state = lax.fori_loop(0, n_quads - 1 + odd_pair, functools.partial(tile_quad, False),
                          (logits_stage(0, sa_ref), (head_init, head_init)))
    carry = lax.cond(odd_pair == 1,
                     lambda st: tile_pair(True, 2 * n_quads, st)[1],
                     lambda st: tile_quad(True, n_quads - 1, st)[1], state)
    for hh in range(2):
        _, acc = carry[hh]
        o_ref[hh * HEAD_DIM:(hh + 1) * HEAD_DIM, :] = (
            acc[:HEAD_DIM] / acc[HEAD_DIM:HEAD_DIM + 1]).astype(BF16)


def _dsa_attention(qt, k, vt, qit, ki, wit, batch, seq_len):
    d, n = qt.shape
    nq = min(ATT_QB, seq_len)
    qpb = seq_len // nq
    n_idx = IDX_HEADS * IDX_DIM
    topk = min(TOPK_MAX, seq_len // 4)
    assert seq_len % (2 * ATT_TS) == 0 and seq_len % nq == 0
    kern = functools.partial(_attn_kernel, seq_len=seq_len, topk=topk)
    return pl.pallas_call(
        kern,
        grid=(batch, qpb, d // LANES),
        in_specs=[
            pl.BlockSpec((LANES, nq), lambda b, i, h: (h, b * qpb + i)),
            pl.BlockSpec((None, seq_len, LANES), lambda b, i, h: (h, b, 0)),
            pl.BlockSpec((LANES, seq_len), lambda b, i, h: (h, b)),
            pl.BlockSpec((n_idx, nq), lambda b, i, h: (0, b * qpb + i)),
            pl.BlockSpec((seq_len, LANES), lambda b, i, h: (b, 0)),
            pl.BlockSpec((IDX_HEADS, nq), lambda b, i, h: (0, b * qpb + i)),
        ],
        out_specs=pl.BlockSpec((LANES, nq), lambda b, i, h: (h, b * qpb + i)),
        out_shape=jax.ShapeDtypeStruct((d, n), BF16),
        scratch_shapes=[pltpu.VMEM((seq_len, nq), I16), pltpu.VMEM((seq_len, nq), I16),
                        pltpu.VMEM((IDX_HEADS, LANES, nq), BF16),
                        pltpu.VMEM((2, ATT_TS, nq), F32), pltpu.VMEM((2, ATT_TS, nq), F32)],
        compiler_params=_cparams("parallel", "parallel", "arbitrary"),
        name="dsa_attention",
    )(qt, k, vt, qit, ki, wit)


OUT_TM = 512


def _out_proj_kernel(x_ref, at_ref, w_ref, gate_ref, o_ref):
    y = lax.dot_general(at_ref[...], w_ref[...], (((0,), (0,)), ((), ())),
                        preferred_element_type=F32)
    o_ref[...] = x_ref[...] + gate_ref[0] * y


def _out_proj(x, a_t, w, gate, seq_len):
    n, d = x.shape
    tm = min(OUT_TM, seq_len)
    tpb = seq_len // tm
    row = pl.BlockSpec((tm, d), lambda i: (i, 0))
    return pl.pallas_call(
        _out_proj_kernel,
        grid=(n // tm,),
        in_specs=[row, pl.BlockSpec((d, tm), lambda i: (0, i)), _const_spec((d, d)), _vec_spec(tpb)],
        out_specs=row,
        out_shape=jax.ShapeDtypeStruct((n, d), F32),
        compiler_params=_cparams("parallel"),
        name="attn_out_proj",
    )(x, a_t, w, gate)


NORMT_TM = 512


def _norm_t_kernel(x_ref, g_ref, sc_ref, sh_ref, o_ref):
    o_ref[...] = _norm_mod(x_ref[...], g_ref[...], sc_ref[0], sh_ref[0]).T


def _norm_transposed(x, norm_g, sc, sh, seq_len):
    n, d = x.shape
    tm = min(NORMT_TM, seq_len)
    tpb = seq_len // tm
    return pl.pallas_call(
        _norm_t_kernel,
        grid=(n // tm,),
        in_specs=[pl.BlockSpec((tm, d), lambda i: (i, 0)), _const_spec((1, d)),
                  _vec_spec(tpb), _vec_spec(tpb)],
        out_specs=pl.BlockSpec((d, tm), lambda i: (0, i)),
        out_shape=jax.ShapeDtypeStruct((d, n), F32),
        compiler_params=_cparams("parallel"),
        name="ssm_norm_t",
    )(x, norm_g, sc, sh)


def _ssm_prep_kernel(are_ref, aim_ref, ldt_ref, brt_ref, bit_ref, cr_ref, ci_ref, crt_ref, cit_ref,
                     toep_ref, bst_ref, cout_ref, lam_ref, kt_ref):
    t_len, p, sg = SSM_T, SSM_STATE, SSM_GROUP
    a_re, a_im = are_ref[0], aim_ref[0]
    dt = jnp.exp(ldt_ref[0])
    ar, th = a_re * dt, a_im * dt
    decay = jnp.exp(ar)
    ab_re, ab_im = decay * jnp.cos(th), decay * jnp.sin(th)
    den = a_re * a_re + a_im * a_im
    nr, ni = ab_re - 1.0, ab_im
    coef_re = (nr * a_re + ni * a_im) / den
    coef_im = (ni * a_re - nr * a_im) / den
    brt, bit = brt_ref[0], bit_ref[0]
    bbr = coef_re * brt - coef_im * bit
    bbi = coef_re * bit + coef_im * brt
    cr, ci = cr_ref[0], ci_ref[0]
    crt, cit = crt_ref[0], cit_ref[0]

    def powers(nn):
        mag = jnp.exp(nn * ar)
        return mag * jnp.cos(nn * th), mag * jnp.sin(nn * th)

    n_iota = lax.broadcasted_iota(I32, (t_len, p), 0).astype(F32)
    l_re, l_im = powers(n_iota)
    l1_re, l1_im = powers(n_iota + 1.0)
    lr_re, lr_im = powers((t_len - 1.0) - n_iota)
    lcat_t = jnp.concatenate([l_re, l_im], axis=1).T
    l1cat_t = jnp.concatenate([l1_re, l1_im], axis=1).T
    l1_re_t, l1_im_t = l1cat_t[:p], l1cat_t[p:]

    mt = []
    for c in range(sg):
        m_re = cr[c:c + 1] * bbr - ci[c:c + 1] * bbi
        m_im = -(cr[c:c + 1] * bbi + ci[c:c + 1] * bbr)
        mt.append(jnp.concatenate([m_re, m_im], axis=1))
    mt = jnp.concatenate(mt, axis=0)
    kt_ref[...] = jnp.dot(mt, lcat_t, preferred_element_type=F32, precision=lax.Precision.HIGHEST)

    tau = lax.broadcasted_iota(I32, (t_len, t_len), 0)
    tt = lax.broadcasted_iota(I32, (t_len, t_len), 1)
    causal = tt >= tau

    def toep_column(c, carry):
        rows = kt_ref[pl.ds(pl.multiple_of(c * sg, sg), sg), :]
        for cp in range(sg):
            krow = jnp.broadcast_to(rows[cp:cp + 1, :], (t_len, t_len))
            tile = pltpu.roll(krow, 0, 1, stride=1, stride_axis=0)
            toep_ref[0, cp * t_len:(cp + 1) * t_len, pl.ds(pl.multiple_of(c * t_len, t_len), t_len)] = (
                jnp.where(causal, tile, 0.0).astype(BF16))
        return carry

    lax.fori_loop(0, sg, toep_column, 0)

    for c in range(sg):
        bst_ref[0, c * t_len:(c + 1) * t_len, :] = jnp.concatenate(
            [lr_re * bbr[c:c + 1] - lr_im * bbi[c:c + 1],
             lr_re * bbi[c:c + 1] + lr_im * bbr[c:c + 1]], axis=1).astype(BF16)
        ccr, cci = crt[:, c:c + 1], cit[:, c:c + 1]
        cout_ref[0, :, c * t_len:(c + 1) * t_len] = jnp.concatenate(
            [ccr * l1_re_t - cci * l1_im_t, -ccr * l1_im_t - cci * l1_re_t], axis=0).astype(BF16)

    lt_re, lt_im = powers(jnp.full((SUBLANES, p), float(t_len), F32))
    lam_ref[0] = jnp.concatenate([lt_re, lt_im], axis=1)


def _ssm_prep(a_re, a_im, log_dt, b_re, b_im, c_re, c_im):
    g, p, sg, t_len = N_GROUPS, SSM_STATE, SSM_GROUP, SSM_T
    vec = lambda a: a.reshape(g, 1, -1)
    spec3 = lambda s: pl.BlockSpec((1,) + s, lambda i: (i, 0, 0))
    return pl.pallas_call(
        _ssm_prep_kernel,
        grid=(g,),
        in_specs=[spec3((1, p)), spec3((1, p)), spec3((1, 1)), spec3((sg, p)), spec3((sg, p)),
                  spec3((sg, p)), spec3((sg, p)), spec3((p, sg)), spec3((p, sg))],
        out_specs=[spec3((sg * t_len, sg * t_len)), spec3((sg * t_len, 2 * p)),
                   spec3((2 * p, sg * t_len)), spec3((SUBLANES, 2 * p))],
        out_shape=[
            jax.ShapeDtypeStruct((g, sg * t_len, sg * t_len), BF16),
            jax.ShapeDtypeStruct((g, sg * t_len, 2 * p), BF16),
            jax.ShapeDtypeStruct((g, 2 * p, sg * t_len), BF16),
            jax.ShapeDtypeStruct((g, SUBLANES, 2 * p), F32),
        ],
        scratch_shapes=[pltpu.VMEM((sg * sg, t_len), F32)],
        compiler_params=_cparams("parallel"),
        name="ssm_prep",
    )(vec(a_re), vec(a_im), vec(log_dt), b_re.transpose(0, 2, 1), b_im.transpose(0, 2, 1),
      c_re, c_im, c_re.transpose(0, 2, 1), c_im.transpose(0, 2, 1))


def _gelu_tanh(x):
    return 0.5 * x * (1.0 + jnp.tanh(math.sqrt(2.0 / math.pi) * (x + 0.044715 * (x * x * x))))


def _ssm_kernel(u_ref, toep_ref, bst_ref, cout_ref, lam_ref, d_ref, o_ref, *, chunks_per_batch):
    sg, t_len, p = SSM_GROUP, SSM_T, SSM_STATE
    nc = u_ref.shape[1]
    lhs = jnp.concatenate([u_ref[c].astype(BF16) for c in range(sg)], axis=1)
    state = _bdot(lhs, bst_ref[0])
    row = lax.broadcasted_iota(I32, (nc, 2 * p), 0) % chunks_per_batch
    lane = lax.broadcasted_iota(I32, (1, 2 * p), 1)
    lam = lam_ref[0]
    a_re = lam[0:1, :p]
    a_im = lam[0:1, p:]

    def cmul_rows(a_re, a_im, x):
        a1 = jnp.concatenate([a_re, a_re], axis=1)
        a2 = jnp.concatenate([-a_im, a_im], axis=1)
        return a1 * x + a2 * pltpu.roll(x, p, 1)

    k = 1
    while k < chunks_per_batch:
        shifted = jnp.where(row >= k, pltpu.roll(state, k, 0), 0.0)
        state = state + cmul_rows(a_re, a_im, shifted)
        a_re, a_im = a_re * a_re - a_im * a_im, 2.0 * a_re * a_im
        k *= 2
    carry = jnp.where(row >= 1, pltpu.roll(state, 1, 0), 0.0)
    del lane
    y = _bdot(lhs, toep_ref[0]) + _bdot(carry.astype(BF16), cout_ref[0])
    for c in range(sg):
        yc = y[:, c * t_len:(c + 1) * t_len] + d_ref[0, c:c + 1, :] * u_ref[c]
        o_ref[c] = _gelu_tanh(yc).astype(BF16)


def _ssm_apply(u_t, toep, bst, cout, lam, d_skip, seq_len):
    d, n = u_t.shape
    g, sg, t_len, p = N_GROUPS, SSM_GROUP, SSM_T, SSM_STATE
    nc = n // t_len
    u3 = u_t.reshape(d, nc, t_len)
    d3 = jnp.broadcast_to(d_skip.reshape(g, sg, 1), (g, sg, t_len))
    kern = functools.partial(_ssm_kernel, chunks_per_batch=seq_len // t_len)
    spec3 = lambda s: pl.BlockSpec((1,) + s, lambda i: (i, 0, 0))
    blk = pl.BlockSpec((sg, nc, t_len), lambda i: (i, 0, 0))
    out = pl.pallas_call(
        kern,
        grid=(g,),
        in_specs=[blk, spec3((sg * t_len, sg * t_len)), spec3((sg * t_len, 2 * p)),
                  spec3((2 * p, sg * t_len)), spec3((SUBLANES, 2 * p)), spec3((sg, t_len))],
        out_specs=blk,
        out_shape=jax.ShapeDtypeStruct((d, nc, t_len), BF16),
        compiler_params=_cparams("parallel"),
        name="ssm_apply",
    )(u3, toep, bst, cout, lam, d3)
    return out.reshape(d, n)


GLU_TM = 512


def _glu_kernel(x_ref, gt_ref, w_ref, gate_ref, o_ref):
    z = lax.dot_general(gt_ref[...], w_ref[...], (((0,), (0,)), ((), ())),
                        preferred_element_type=F32)
    d = x_ref.shape[1]
    o_ref[...] = x_ref[...] + gate_ref[0] * (z[:, :d] * _sigmoid(z[:, d:]))


def _glu_residual(x, g_t, w_glu, gate, seq_len):
    n, d = x.shape
    tm = min(GLU_TM, seq_len)
    tpb = seq_len // tm
    row = pl.BlockSpec((tm, d), lambda i: (i, 0))
    return pl.pallas_call(
        _glu_kernel,
        grid=(n // tm,),
        in_specs=[row, pl.BlockSpec((d, tm), lambda i: (0, i)), _const_spec((d, 2 * d)),
                  _vec_spec(tpb)],
        out_specs=row,
        out_shape=jax.ShapeDtypeStruct((n, d), F32),
        compiler_params=_cparams("parallel"),
        name="ssm_glu",
    )(x, g_t, w_glu, gate)


def kernel(x, c, positions, ada_w, ada_b, norm_mix, norm_ffn, attn_w_in, attn_q_gain, attn_k_gain,
           attn_w_out, ssm_a_re, ssm_a_im, ssm_log_dt, ssm_b_re, ssm_b_im, ssm_c_re, ssm_c_im,
           ssm_d, ssm_w_glu, ffn_w_up, ffn_conv_w, ffn_conv_b, ffn_w_down):
    batch, seq_len, d = x.shape
    depth = ada_w.shape[0]
    n = batch * seq_len
    xs = x.reshape(n, d)
    mod = _modulation(c, ada_w, ada_b)
    cos_t, sin_a, sin_b = _rope_tables(positions)
    for i in range(depth):
        sh_m, sc_m, g_m, sh_f, sc_f, g_f = [mod[i, :, k] for k in range(6)]
        j = i // 2
        if i % 2 == 0:
            w_in = jnp.pad(attn_w_in[j], ((0, 0), (0, IN_COLS_PAD - IN_COLS))).astype(BF16)
            qt, k, vt, qit, ki, wit = _attn_proj(
                xs, norm_mix[i].reshape(1, d), sc_m, sh_m, w_in, attn_q_gain[j], attn_k_gain[j],
                cos_t, sin_a, sin_b, seq_len)
            o = _dsa_attention(qt, k, vt, qit, ki, wit, batch, seq_len)
            xs = _out_proj(xs, o, attn_w_out[j].astype(BF16), g_m, seq_len)
        else:
            u_t = _norm_transposed(xs, norm_mix[i].reshape(1, d), sc_m, sh_m, seq_len)
            toep, bst, cout, lam = _ssm_prep(ssm_a_re[j], ssm_a_im[j], ssm_log_dt[j], ssm_b_re[j],
                                             ssm_b_im[j], ssm_c_re[j], ssm_c_im[j])
            g_t = _ssm_apply(u_t, toep, bst, cout, lam, ssm_d[j], seq_len)
            xs = _glu_residual(xs, g_t, ssm_w_glu[j].astype(BF16), g_m, seq_len)
        w_up = ffn_w_up[i].astype(BF16)
        xs = _conv_ffn(xs, norm_ffn[i].reshape(1, d), sc_f, sh_f, g_f, w_up[:, :D_FF], w_up[:, D_FF:],
                       ffn_conv_w[i], ffn_conv_b[i].reshape(1, -1), ffn_w_down[i].astype(BF16),
                       seq_len)
    return xs.reshape(batch, seq_len, d)
```
